```python
import jax, jax.numpy as jnp
from jax import lax
import numpy as np

D_MODEL = 2048
BATCH = 4
SEQ = 2048
DEPTH = 4
DEC_BATCH = 128
DEC_SEQ = 4
PAST_LEN = 8192
PAGE_SIZE = 128

H_A = 8
V_DIM = 128
NOPE_DIM = 128
ROPE_DIM = 64
KV_LORA = 256
Q_LORA = 512
H_B = 8
HEAD_DIM_B = 128
ROT_B = HEAD_DIM_B // 4
N_IDX = 16
D_IDX = 64
ROT_IDX = D_IDX // 4
TOPK_MAX = 256
MIX_WIDTH = H_A * V_DIM + H_B * HEAD_DIM_B
IN_WIDTH = Q_LORA + KV_LORA + ROPE_DIM + H_A * V_DIM + H_B * HEAD_DIM_B + 2 * HEAD_DIM_B + N_IDX * D_IDX + D_IDX + N_IDX + H_B * HEAD_DIM_B
THETA_PARTIAL = 500000.0
THETA_MLA = 10000.0
MLA_SCALE = (NOPE_DIM + ROPE_DIM) ** -0.5
DSA_SCALE = HEAD_DIM_B ** -0.5
IDX_SCALE = D_IDX ** -0.5
IDX_W_SCALE = N_IDX ** -0.5
Q_BLOCK = 128
EPS = 1e-6
NEG = -1e30

kernel_name = 'hymba_mla_dsa_gated_decode_step'


def rmsnorm(x, g):
    xf = x.astype(jnp.float32)
    y = xf * lax.rsqrt(jnp.mean(xf * xf, axis=-1, keepdims=True) + EPS)
    return (y * g.astype(jnp.float32)).astype(x.dtype)


def rope(x, pos, n_rot, theta):
    half = n_rot // 2
    freqs = jnp.power(jnp.float32(theta), -jnp.arange(half, dtype=jnp.float32) / half)
    ang = pos.astype(jnp.float32)[:, None] * freqs[None, :]
    ang = ang.reshape((pos.shape[0],) + (1,) * (x.ndim - 3) + (half,))
    cos, sin = jnp.cos(ang), jnp.sin(ang)
    xr = x[..., :n_rot].astype(jnp.float32)
    x1, x2 = xr[..., :half], xr[..., half:]
    rot = jnp.concatenate([x1 * cos - x2 * sin, x2 * cos + x1 * sin], axis=-1).astype(x.dtype)
    return jnp.concatenate([rot, x[..., n_rot:]], axis=-1)


def split_columns(z):
    sizes = (Q_LORA, KV_LORA, ROPE_DIM, H_A * V_DIM,
             H_B * HEAD_DIM_B, HEAD_DIM_B, HEAD_DIM_B, N_IDX * D_IDX, D_IDX, N_IDX, H_B * HEAD_DIM_B)
    out, off = [], 0
    for s in sizes:
        out.append(z[..., off:off + s])
        off += s
    return out


def prep(x, pos, g_attn, w_in, g_q, w_uq, g_kv, w_uk):
    b, t, _ = x.shape
    h = rmsnorm(x, g_attn)
    cq, ckv, kr, gate_a, qb, kb, vb, iq, ik, iw, gate_b = split_columns(h @ w_in)
    q = (rmsnorm(cq, g_q) @ w_uq).reshape(b, t, H_A, NOPE_DIM + ROPE_DIM)
    q_lat = jnp.einsum('bthn,hnr->bthr', q[..., :NOPE_DIM], w_uk)
    q_rope = rope(q[..., NOPE_DIM:], pos, ROPE_DIM, THETA_MLA)
    ckv = rmsnorm(ckv, g_kv)
    kr = rope(kr, pos, ROPE_DIM, THETA_MLA)
    qb = rope(qb.reshape(b, t, H_B, HEAD_DIM_B), pos, ROT_B, THETA_PARTIAL)
    kb = rope(kb, pos, ROT_B, THETA_PARTIAL)
    iq = rope(iq.reshape(b, t, N_IDX, D_IDX), pos, ROT_IDX, THETA_PARTIAL)
    ik = rope(ik, pos, ROT_IDX, THETA_PARTIAL)
    return q_lat, q_rope, ckv, kr, gate_a, qb, kb, vb, iq, ik, iw, gate_b


def mla_attend(q_lat, q_rope, ckv, kr, q_pos, k_pos):
    s = (jnp.einsum('bthr,bsr->bhts', q_lat, ckv)
         + jnp.einsum('bthp,bsp->bhts', q_rope, kr)).astype(jnp.float32) * MLA_SCALE
    s = jnp.where(k_pos[None, :] <= q_pos[:, None], s, NEG)
    p = jax.nn.softmax(s, axis=-1).astype(ckv.dtype)
    return jnp.einsum('bhts,bsr->bthr', p, ckv)


def indexer_scores(iq, iw, ik, q_pos, k_pos):
    dots = jnp.einsum('btnd,bsd->btns', iq, ik).astype(jnp.float32) * IDX_SCALE
    sc = jnp.einsum('btns,btn->bts', jax.nn.relu(dots), iw.astype(jnp.float32) * IDX_W_SCALE)
    return jnp.where(k_pos[None, None, :] <= q_pos[None, :, None], sc, -jnp.inf)


def take_rows(a, idx):
    return jax.vmap(lambda ab, ib: ab[ib])(a, idx)


def sparse_attend(q, k_sel, v_sel, valid):
    s = jnp.einsum('bthd,btkd->bhtk', q, k_sel).astype(jnp.float32) * DSA_SCALE
    s = jnp.where(valid[:, None], s, NEG)
    p = jax.nn.softmax(s, axis=-1).astype(v_sel.dtype)
    return jnp.einsum('bhtk,btkd->bthd', p, v_sel)


def combine(x, o_lat, o_b, gate_a, gate_b, w_uv, w_out):
    b, t, _ = x.shape
    o_a = jnp.einsum('bthr,hrv->bthv', o_lat, w_uv).reshape(b, t, H_A * V_DIM)
    mix = jnp.concatenate([o_a * jax.nn.silu(gate_a),
                           o_b.reshape(b, t, H_B * HEAD_DIM_B) * jax.nn.silu(gate_b)], axis=-1)
    return x + mix @ w_out


def prompt_layer(x, g_attn, w_in, g_q, w_uq, g_kv, w_uk, w_uv, w_out):
    b, t, _ = x.shape
    pos = jnp.arange(t)
    q_lat, q_rope, ckv, kr, gate_a, qb, kb, vb, iq, ik, iw, gate_b = prep(x, pos, g_attn, w_in, g_q, w_uq, g_kv, w_uk)
    n_blk = t // Q_BLOCK
    k_top = min(TOPK_MAX, t // 4)

    def to_blocks(a):
        return a.reshape((b, n_blk, Q_BLOCK) + a.shape[2:]).swapaxes(0, 1)

    def from_blocks(a):
        return a.swapaxes(0, 1).reshape((b, t) + a.shape[3:])

    def block(args):
        i, ql, qr, qbb, iqb, iwb = args
        q_pos = i * Q_BLOCK + jnp.arange(Q_BLOCK)
        o_lat = mla_attend(ql, qr, ckv, kr, q_pos, pos)
        _, idx = lax.top_k(indexer_scores(iqb, iwb, ik, q_pos, pos), k_top)
        valid = idx <= q_pos[None, :, None]
        o_b = sparse_attend(qbb, take_rows(kb, idx), take_rows(vb, idx), valid)
        return o_lat, o_b

    o_lat, o_b = lax.map(block, (jnp.arange(n_blk), to_blocks(q_lat), to_blocks(q_rope),
                                 to_blocks(qb), to_blocks(iq), to_blocks(iw)))
    y = combine(x, from_blocks(o_lat), from_blocks(o_b), gate_a, gate_b, w_uv, w_out)
    return y, (ckv, kr, kb, vb, ik)


def sample_layer(x, layer, cache_mla_ckv, cache_mla_krope, cache_dsa_k, cache_dsa_v, cache_dsa_idxk,
                 page_table, g_attn, w_in, g_q, w_uq, g_kv, w_uk, w_uv, w_out):
    b, t, _ = x.shape
    past = page_table.shape[1] * PAGE_SIZE
    q_pos = past + jnp.arange(t)
    k_pos = jnp.arange(past + t)
    q_lat, q_rope, ckv, kr, gate_a, qb, kb, vb, iq, ik, iw, gate_b = prep(x, q_pos, g_attn, w_in, g_q, w_uq, g_kv, w_uk)

    def full_past(pool, new):
        return jnp.concatenate([pool[layer, page_table].reshape(b, past, pool.shape[-1]), new], axis=1)

    o_lat = mla_attend(q_lat, q_rope, full_past(cache_mla_ckv, ckv), full_past(cache_mla_krope, kr), q_pos, k_pos)
    k_top = min(TOPK_MAX, (past + t) // 4)
    _, idx = lax.top_k(indexer_scores(iq, iw, full_past(cache_dsa_idxk, ik), q_pos, k_pos), k_top)
    in_past = (idx < past)[..., None]
    pidx = jnp.minimum(idx, past - 1)
    phys = take_rows(page_table, pidx // PAGE_SIZE)
    off = pidx % PAGE_SIZE
    nidx = jnp.clip(idx - past, 0, t - 1)
    k_sel = jnp.where(in_past, cache_dsa_k[layer, phys, off], take_rows(kb, nidx))
    v_sel = jnp.where(in_past, cache_dsa_v[layer, phys, off], take_rows(vb, nidx))
    o_b = sparse_attend(qb, k_sel, v_sel, idx <= q_pos[None, :, None])
    y = combine(x, o_lat, o_b, gate_a, gate_b, w_uv, w_out)
    return y, (ckv, kr, kb, vb, ik)


def setup_inputs(seed: int = 0) -> dict:
    key = jax.random.key(seed)
    ks = jax.random.split(key, 20)
    n_pages = PAST_LEN // PAGE_SIZE
    n_used = DEC_BATCH * n_pages
    n_pool = n_used + max(1, n_used // 4)
    f32 = jnp.float32
    nrm = jax.random.normal
    page_table = jax.random.permutation(ks[0], n_pool)[:n_used].reshape(DEC_BATCH, n_pages).astype(jnp.int32)
    return {
        'x_prompt': nrm(ks[1], (BATCH, SEQ, D_MODEL), f32),
        'x_sample': nrm(ks[2], (DEC_BATCH, DEC_SEQ, D_MODEL), f32),
        'cache_mla_ckv': nrm(ks[3], (DEPTH, n_pool, PAGE_SIZE, KV_LORA), f32),
        'cache_mla_krope': nrm(ks[4], (DEPTH, n_pool, PAGE_SIZE, ROPE_DIM), f32),
        'cache_dsa_k': nrm(ks[5], (DEPTH, n_pool, PAGE_SIZE, HEAD_DIM_B), f32),
        'cache_dsa_v': nrm(ks[6], (DEPTH, n_pool, PAGE_SIZE, HEAD_DIM_B), f32),
        'cache_dsa_idxk': nrm(ks[7], (DEPTH, n_pool, PAGE_SIZE, D_IDX), f32),
        'page_table': page_table,
        'g_attn': 1.0 + 0.02 * nrm(ks[8], (DEPTH, D_MODEL), f32),
        'w_in': nrm(ks[9], (DEPTH, D_MODEL, IN_WIDTH), f32) * D_MODEL ** -0.5,
        'g_q': 1.0 + 0.02 * nrm(ks[10], (DEPTH, Q_LORA), f32),
        'w_uq': nrm(ks[11], (DEPTH, Q_LORA, H_A * (NOPE_DIM + ROPE_DIM)), f32) * Q_LORA ** -0.5,
        'g_kv': 1.0 + 0.02 * nrm(ks[12], (DEPTH, KV_LORA), f32),
        'w_uk': nrm(ks[13], (DEPTH, H_A, NOPE_DIM, KV_LORA), f32) * NOPE_DIM ** -0.5,
        'w_uv': nrm(ks[14], (DEPTH, H_A, KV_LORA, V_DIM), f32) * KV_LORA ** -0.5,
        'w_out': nrm(ks[15], (DEPTH, MIX_WIDTH, D_MODEL), f32) * MIX_WIDTH ** -0.5,
        'g_final': 1.0 + 0.02 * nrm(ks[16], (D_MODEL,), f32),
    }


def reference(x_prompt, x_sample, cache_mla_ckv, cache_mla_krope, cache_dsa_k, cache_dsa_v, cache_dsa_idxk,
              page_table, g_attn, w_in, g_q, w_uq, g_kv, w_uk, w_uv, w_out, g_final):
    xp, xs = x_prompt, x_sample
    p_rows, s_rows = [], []
    for l in range(DEPTH):
        xp, pr = prompt_layer(xp, g_attn[l], w_in[l], g_q[l], w_uq[l], g_kv[l], w_uk[l], w_uv[l], w_out[l])
        xs, sr = sample_layer(xs, l, cache_mla_ckv, cache_mla_krope, cache_dsa_k, cache_dsa_v, cache_dsa_idxk,
                              page_table, g_attn[l], w_in[l], g_q[l], w_uq[l], g_kv[l], w_uk[l], w_uv[l], w_out[l])
        p_rows.append(pr)
        s_rows.append(sr)
    y_prompt = rmsnorm(xp, g_final)
    y_sample = rmsnorm(xs, g_final)
    p_ckv = jnp.stack([r[0] for r in p_rows])
    p_krope = jnp.stack([r[1] for r in p_rows])
    p_k = jnp.stack([r[2] for r in p_rows])
    p_v = jnp.stack([r[3] for r in p_rows])
    p_idxk = jnp.stack([r[4] for r in p_rows])
    s_ckv = jnp.stack([r[0] for r in s_rows])
    s_krope = jnp.stack([r[1] for r in s_rows])
    s_k = jnp.stack([r[2] for r in s_rows])
    s_v = jnp.stack([r[3] for r in s_rows])
    s_idxk = jnp.stack([r[4] for r in s_rows])
    return (y_prompt, y_sample, p_ckv, p_krope, p_k, p_v, p_idxk, s_ckv, s_krope, s_k, s_v, s_idxk)
```

```python
import functools

import jax
import jax.numpy as jnp
from jax import lax
from jax.experimental import pallas as pl
from jax.experimental.pallas import tpu as pltpu

F32 = jnp.float32
BF16 = jnp.bfloat16
I32 = jnp.int32

D_MODEL = 2048
BATCH = 4
SEQ = 2048
DEPTH = 4
DEC_BATCH = 128
DEC_SEQ = 4
PAST_LEN = 8192
PAGE_SIZE = 128
H_A = 8
V_DIM = 128
NOPE_DIM = 128
ROPE_DIM = 64
KV_LORA = 256
Q_LORA = 512
H_B = 8
HEAD_DIM_B = 128
ROT_B = HEAD_DIM_B // 4
N_IDX = 16
D_IDX = 64
ROT_IDX = D_IDX // 4
TOPK_MAX = 256
THETA_PARTIAL = 500000.0
THETA_MLA = 10000.0
MLA_SCALE = (NOPE_DIM + ROPE_DIM) ** -0.5
DSA_SCALE = HEAD_DIM_B ** -0.5
IDX_SCALE = D_IDX ** -0.5
IDX_W_SCALE = N_IDX ** -0.5
EPS = 1e-6
NEG = -1e30

LANES = 128
N_PROMPT = BATCH * SEQ
N_SAMPLE = DEC_BATCH * DEC_SEQ
N_ROWS = N_PROMPT + N_SAMPLE
QB = 128
N_QBLK = N_ROWS // QB
T_PAD = 8

OFF_A = 0
OFF_QB = 1024
OFF_IQ = 2048
OFF_GA = 3072
OFF_GB = 4096
OFF_KV = 5120
Z_WIDTH = 5376

INT_MIN = -(2 ** 31)
VMEM_LIMIT = 52 * 1024 * 1024


def _nt(a, b):
    return lax.dot_general(a, b, (((1,), (1,)), ((), ())), preferred_element_type=F32)


def _rms(x, g):
    var = jnp.mean(x * x, axis=-1, keepdims=True)
    return (x * lax.rsqrt(var + EPS)) * g


def _rope3(x, c, sa, sb, half):
    return x * c + pltpu.roll(x, LANES - half, 1) * sa + pltpu.roll(x, half, 1) * sb


TM_IN = 512
TN_IN = 768


def _in_proj_kernel(x_ref, g_ref, w_ref, z_ref, h_sc):
    @pl.when(pl.program_id(1) == 0)
    def _():
        h_sc[...] = _rms(x_ref[...], g_ref[...]).astype(BF16)

    z_ref[...] = jnp.dot(h_sc[...], w_ref[...], preferred_element_type=F32)


def _in_proj(x, g, w):
    return pl.pallas_call(
        _in_proj_kernel,
        out_shape=jax.ShapeDtypeStruct((N_ROWS, Z_WIDTH), F32),
        grid=(N_ROWS // TM_IN, Z_WIDTH // TN_IN),
        in_specs=[
            pl.BlockSpec((TM_IN, D_MODEL), lambda i, j: (i, 0)),
            pl.BlockSpec((1, D_MODEL), lambda i, j: (0, 0)),
            pl.BlockSpec((D_MODEL, TN_IN), lambda i, j: (0, j)),
        ],
        out_specs=pl.BlockSpec((TM_IN, TN_IN), lambda i, j: (i, j)),
        scratch_shapes=[pltpu.VMEM((TM_IN, D_MODEL), BF16)],
        compiler_params=pltpu.CompilerParams(
            dimension_semantics=("parallel", "arbitrary"), vmem_limit_bytes=VMEM_LIMIT),
        name="in_proj",
    )(x, g, w)


TM_POST = 256
R_POST = TM_POST // QB
T_CM, T_SAM, T_SBM, T_CB, T_SAB, T_SBB, T_CX, T_SAX, T_SBX, T_CK, T_SAMK, T_SBMK, T_SAXK, T_SBXK = range(14)
N_TAB = 14


def _post_kernel(za_ref, zqb_ref, ziq_ref, zkv_ref, tab_ref, gq_ref, gkv_ref, wuq_ref, wuk_ref,
                 ckv_o, krik_o, kb_o, vb_o, ckvb_o, krikb_o, kbb_o, vbb_o,
                 ql_o, qr_o, qb_o, iq_o, iw_o):
    lane = lax.broadcasted_iota(I32, (TM_POST, LANES), 1)
    lo_half = lane < 64

    cqn = _rms(za_ref[:, 0:Q_LORA], gq_ref[...]).astype(BF16)
    q = jnp.dot(cqn, wuq_ref[...], preferred_element_type=F32)
    for h in range(H_A):
        qn = q[:, h * NOPE_DIM:(h + 1) * NOPE_DIM].astype(BF16)
        qlat = jnp.dot(qn, wuk_ref[h], preferred_element_type=F32).astype(BF16)
        for r in range(R_POST):
            ql_o[r, h] = qlat[r * QB:(r + 1) * QB]
    c_m, sa_m, sb_m = tab_ref[T_CM], tab_ref[T_SAM], tab_ref[T_SBM]
    for j in range(H_A * ROPE_DIM // LANES):
        x = q[:, H_A * NOPE_DIM + j * LANES:H_A * NOPE_DIM + (j + 1) * LANES]
        y = _rope3(x, c_m, sa_m, sb_m, ROPE_DIM // 2)
        y_lo = jnp.where(lo_half, y, 0.0).astype(BF16)
        y_hi = jnp.where(lo_half, pltpu.roll(y, 64, 1), 0.0).astype(BF16)
        for r in range(R_POST):
            qr_o[r, 2 * j] = y_lo[r * QB:(r + 1) * QB]
            qr_o[r, 2 * j + 1] = y_hi[r * QB:(r + 1) * QB]

    ckv = _rms(za_ref[:, Q_LORA:Q_LORA + KV_LORA], gkv_ref[...])
    ckv_o[...] = ckv
    ckvb_o[...] = ckv.astype(BF16)
    x = za_ref[:, 768:896]
    krik = (x * tab_ref[T_CK]
            + pltpu.roll(x, LANES - ROPE_DIM // 2, 1) * tab_ref[T_SAMK]
            + pltpu.roll(x, ROPE_DIM // 2, 1) * tab_ref[T_SBMK]
            + pltpu.roll(x, LANES - ROT_IDX // 2, 1) * tab_ref[T_SAXK]
            + pltpu.roll(x, ROT_IDX // 2, 1) * tab_ref[T_SBXK])
    krik_o[...] = krik
    krikb_o[...] = krik.astype(BF16)
    iw_o[...] = za_ref[:, 896:1024]

    c_b, sa_b, sb_b = tab_ref[T_CB], tab_ref[T_SAB], tab_ref[T_SBB]
    for h in range(H_B):
        y = _rope3(zqb_ref[:, h * LANES:(h + 1) * LANES], c_b, sa_b, sb_b, ROT_B // 2).astype(BF16)
        for r in range(R_POST):
            qb_o[r, h] = y[r * QB:(r + 1) * QB]
    kb = _rope3(zkv_ref[:, 0:LANES], c_b, sa_b, sb_b, ROT_B // 2)
    kb_o[...] = kb
    kbb_o[...] = kb.astype(BF16)
    vb = zkv_ref[:, LANES:2 * LANES]
    vb_o[...] = vb
    vbb_o[...] = vb.astype(BF16)

    c_x, sa_x, sb_x = tab_ref[T_CX], tab_ref[T_SAX], tab_ref[T_SBX]
    for j in range(N_IDX * D_IDX // LANES):
        y = _rope3(ziq_ref[:, j * LANES:(j + 1) * LANES], c_x, sa_x, sb_x, ROT_IDX // 2)
        y_even = jnp.where(lo_half, 0.0, pltpu.roll(y, 64, 1)).astype(BF16)
        y_odd = jnp.where(lo_half, 0.0, y).astype(BF16)
        for r in range(R_POST):
            iq_o[r, 2 * j] = y_even[r * QB:(r + 1) * QB]
            iq_o[r, 2 * j + 1] = y_odd[r * QB:(r + 1) * QB]


def _post_proj(z, tabs, gq, gkv, wuq, wuk):
    row = lambda w: pl.BlockSpec((TM_POST, w), lambda i: (i, 0))
    blk = lambda nh: pl.BlockSpec((R_POST, nh, QB, LANES), lambda i: (i, 0, 0, 0))
    out_shape = (
        jax.ShapeDtypeStruct((N_ROWS, KV_LORA), F32),
        jax.ShapeDtypeStruct((N_ROWS, LANES), F32),
        jax.ShapeDtypeStruct((N_ROWS, LANES), F32),
        jax.ShapeDtypeStruct((N_ROWS, LANES), F32),
        jax.ShapeDtypeStruct((N_ROWS, KV_LORA), BF16),
        jax.ShapeDtypeStruct((N_ROWS, LANES), BF16),
        jax.ShapeDtypeStruct((N_ROWS, LANES), BF16),
        jax.ShapeDtypeStruct((N_ROWS, LANES), BF16),
        jax.ShapeDtypeStruct((N_QBLK, H_A, QB, KV_LORA), BF16),
        jax.ShapeDtypeStruct((N_QBLK, H_A, QB, LANES), BF16),
        jax.ShapeDtypeStruct((N_QBLK, H_B, QB, LANES), BF16),
        jax.ShapeDtypeStruct((N_QBLK, N_IDX, QB, LANES), BF16),
        jax.ShapeDtypeStruct((N_ROWS, LANES), F32),
    )
    out_specs = (
        row(KV_LORA), row(LANES), row(LANES), row(LANES),
        row(KV_LORA), row(LANES), row(LANES), row(LANES),
        pl.BlockSpec((R_POST, H_A, QB, KV_LORA), lambda i: (i, 0, 0, 0)),
        blk(H_A), blk(H_B), blk(N_IDX), row(LANES),
    )
    return pl.pallas_call(
        _post_kernel,
        out_shape=out_shape,
        grid=(N_ROWS // TM_POST,),
        in_specs=[
            pl.BlockSpec((TM_POST, 1024), lambda i: (i, OFF_A // 1024)),
            pl.BlockSpec((TM_POST, 1024), lambda i: (i, OFF_QB // 1024)),
            pl.BlockSpec((TM_POST, 1024), lambda i: (i, OFF_IQ // 1024)),
            pl.BlockSpec((TM_POST, 256), lambda i: (i, OFF_KV // 256)),
            pl.BlockSpec((N_TAB, TM_POST, LANES), lambda i: (0, i, 0)),
            pl.BlockSpec((1, Q_LORA), lambda i: (0, 0)),
            pl.BlockSpec((1, KV_LORA), lambda i: (0, 0)),
            pl.BlockSpec((Q_LORA, H_A * (NOPE_DIM + ROPE_DIM)), lambda i: (0, 0)),
            pl.BlockSpec((H_A, NOPE_DIM, KV_LORA), lambda i: (0, 0, 0)),
        ],
        out_specs=out_specs,
        compiler_params=pltpu.CompilerParams(
            dimension_semantics=("parallel",), vmem_limit_bytes=VMEM_LIMIT),
        name="post_proj",
    )(z, z, z, z, tabs, gq, gkv, wuq, wuk)


TK_MLA = 256
N_QB_SEQ = SEQ // QB


def _mla_prompt_kernel(ql_ref, qr_ref, ckv_ref, krik_ref, o_ref, m_sc, l_sc, acc_sc):
    i = pl.program_id(1)
    ql = ql_ref[0].reshape(H_A * QB, KV_LORA)
    qr = qr_ref[0].reshape(H_A * QB, LANES)
    m_sc[...] = jnp.full(m_sc.shape, -jnp.inf, F32)
    l_sc[...] = jnp.zeros(l_sc.shape, F32)
    acc_sc[...] = jnp.zeros(acc_sc.shape, F32)

    def step(c, masked):
        k0 = pl.multiple_of(c * TK_MLA, TK_MLA)
        kc = ckv_ref[pl.ds(k0, TK_MLA), :]
        kr = krik_ref[pl.ds(k0, TK_MLA), :]
        s = (_nt(ql, kc) + _nt(qr, kr)) * MLA_SCALE
        if masked:
            t = lax.broadcasted_iota(I32, s.shape, 0) & (QB - 1)
            kpos = k0 + lax.broadcasted_iota(I32, s.shape, 1)
            s = jnp.where(kpos <= i * QB + t, s, NEG)
        m_prev = m_sc[...]
        m_new = jnp.maximum(m_prev, jnp.max(s, axis=-1, keepdims=True))
        alpha = jnp.exp(m_prev - m_new)
        p = jnp.exp(s - m_new)
        l_sc[...] = alpha * l_sc[...] + jnp.sum(p, axis=-1, keepdims=True)
        acc_sc[...] = alpha * acc_sc[...] + jnp.dot(p.astype(BF16), kc, preferred_element_type=F32)
        m_sc[...] = m_new

    n_full = i // 2

    def body(c, carry):
        step(c, False)
        return carry

    lax.fori_loop(0, n_full, body, 0)
    step(n_full, True)
    o = acc_sc[...] / l_sc[...]
    for h in range(H_A):
        o_ref[:, h * KV_LORA:(h + 1) * KV_LORA] = o[h * QB:(h + 1) * QB].astype(BF16)


def _mla_prompt(ql, qr, ckvb, krikb):
    return pl.pallas_call(
        _mla_prompt_kernel,
        out_shape=jax.ShapeDtypeStruct((N_PROMPT, H_A * KV_LORA), BF16),
        grid=(BATCH, N_QB_SEQ),
        in_specs=[
            pl.BlockSpec((1, H_A, QB, KV_LORA), lambda b, i: (b * N_QB_SEQ + i, 0, 0, 0)),
            pl.BlockSpec((1, H_A, QB, LANES), lambda b, i: (b * N_QB_SEQ + i, 0, 0, 0)),
            pl.BlockSpec((SEQ, KV_LORA), lambda b, i: (b, 0)),
            pl.BlockSpec((SEQ, LANES), lambda b, i: (b, 0)),
        ],
        out_specs=pl.BlockSpec((QB, H_A * KV_LORA), lambda b, i: (b * N_QB_SEQ + i, 0)),
        scratch_shapes=[
            pltpu.VMEM((H_A * QB, 1), F32),
            pltpu.VMEM((H_A * QB, 1), F32),
            pltpu.VMEM((H_A * QB, KV_LORA), F32),
        ],
        compiler_params=pltpu.CompilerParams(
            dimension_semantics=("parallel", "arbitrary"), vmem_limit_bytes=VMEM_LIMIT),
        name="mla_prompt",
    )(ql, qr, ckvb, krikb)


def _count(mask):
    return jnp.sum(jnp.where(mask, 1.0, 0.0), axis=-1, keepdims=True)


def _topk_select(scores, k, idx_bits):
    rows = scores.shape[0]
    bits = lax.bitcast_convert_type(scores, I32)
    key = bits ^ ((bits >> 31) & 0x7FFFFFFF)
    kf = jnp.float32(k)

    base = jnp.full((rows, 1), INT_MIN, I32)
    cand = jnp.zeros((rows, 1), I32)
    base = jnp.where(_count(key >= cand) >= kf, cand, base)

    def value_bit(n, base):
        cand = base | (jnp.int32(1) << (30 - n))
        return jnp.where(_count(key >= cand) >= kf, cand, base)

    thr = lax.fori_loop(0, 31, value_bit, base)
    gt = key > thr
    eq = key == thr
    need = kf - _count(gt)
    idx = lax.broadcasted_iota(I32, scores.shape, 1)

    def index_bit(n, cut):
        cand = cut | (jnp.int32(1) << (idx_bits - 1 - n))
        below = jnp.where(eq, jnp.where(idx < cand, 1.0, 0.0), 0.0)
        return jnp.where(jnp.sum(below, axis=-1, keepdims=True) <= need, cand, cut)

    cut = lax.fori_loop(0, idx_bits, index_bit, jnp.zeros((rows, 1), I32))
    return jnp.where(gt, 1.0, jnp.where(eq, jnp.where(idx < cut, 1.0, 0.0), 0.0))


K_TOP_PROMPT = min(TOPK_MAX, SEQ // 4)
IDX_BITS_PROMPT = 11


def _dsa_prompt_kernel(iq_ref, iw_ref, qb_ref, krik_ref, kb_ref, vb_ref, o_ref, i_sc, bias_sc):
    i = pl.program_id(1)
    krik = krik_ref[...]
    w_all = iw_ref[...] * (IDX_SCALE * IDX_W_SCALE)
    for h in range(N_IDX):
        d = _nt(iq_ref[0, h], krik)
        term = jnp.maximum(d, 0.0) * w_all[:, h:h + 1]
        if h == 0:
            i_sc[...] = term
        else:
            i_sc[...] += term
    qpos = i * QB + lax.broadcasted_iota(I32, (QB, SEQ), 0)
    kpos = lax.broadcasted_iota(I32, (QB, SEQ), 1)
    causal = kpos <= qpos
    scores = jnp.where(causal, i_sc[...], -jnp.inf)
    sel = _topk_select(scores, K_TOP_PROMPT, IDX_BITS_PROMPT)
    bias_sc[...] = jnp.where(causal, (sel - 1.0) * -NEG, NEG)

    kb = kb_ref[...]
    vb = vb_ref[...]
    for h in range(H_B):
        s = _nt(qb_ref[0, h], kb) * DSA_SCALE + bias_sc[...]
        m = jnp.max(s, axis=-1, keepdims=True)
        p = jnp.exp(s - m)
        l = jnp.sum(p, axis=-1, keepdims=True)
        o = jnp.dot(p.astype(BF16), vb, preferred_element_type=F32) / l
        o_ref[:, h * HEAD_DIM_B:(h + 1) * HEAD_DIM_B] = o


def _dsa_prompt(iq, iw, qb, krikb, kbb, vbb):
    blk = lambda nh: pl.BlockSpec((1, nh, QB, LANES), lambda b, i: (b * N_QB_SEQ + i, 0, 0, 0))
    seq = pl.BlockSpec((SEQ, LANES), lambda b, i: (b, 0))
    return pl.pallas_call(
        _dsa_prompt_kernel,
        out_shape=jax.ShapeDtypeStruct((N_PROMPT, H_B * HEAD_DIM_B), F32),
        grid=(BATCH, N_QB_SEQ),
        in_specs=[
            blk(N_IDX),
            pl.BlockSpec((QB, LANES), lambda b, i: (b * N_QB_SEQ + i, 0)),
            blk(H_B), seq, seq, seq,
        ],
        out_specs=pl.BlockSpec((QB, H_B * HEAD_DIM_B), lambda b, i: (b * N_QB_SEQ + i, 0)),
        scratch_shapes=[pltpu.VMEM((QB, SEQ), F32), pltpu.VMEM((QB, SEQ), F32)],
        compiler_params=pltpu.CompilerParams(
            dimension_semantics=("parallel", "arbitrary"), vmem_limit_bytes=VMEM_LIMIT),
        name="dsa_prompt",
    )(iq, iw, qb, krikb, kbb, vbb)


PAGES_PER_CHUNK = 16
CHUNK = PAGES_PER_CHUNK * PAGE_SIZE
N_CHUNK = PAST_LEN // CHUNK
N_PAGES = PAST_LEN // PAGE_SIZE
S_ALL = PAST_LEN + LANES
K_TOP_SAMPLE = min(TOPK_MAX, (PAST_LEN + DEC_SEQ) // 4)
IDX_BITS_SAMPLE = 14
ROWS_A = H_A * T_PAD
ROWS_I = N_IDX * T_PAD


def _sample_kernel(pt_ref, l_ref,
                   ql_ref, qr_ref, iq_ref, iw_ref, qb_ref,
                   ckvn_ref, krn_ref, ikn_ref, kbn_ref, vbn_ref,
                   c_ckv, c_kr, c_ik, c_k, c_v,
                   ol_ref, ob_ref,
                   ckv_buf, kr_buf, ik_buf, k_buf, v_buf, sem,
                   i_all, sb_all, v_all):
    b = pl.program_id(0)
    layer = l_ref[0]
    caches = ((c_ckv, ckv_buf), (c_kr, kr_buf), (c_ik, ik_buf), (c_k, k_buf), (c_v, v_buf))

    def chunk_copies(bb, c, slot):
        copies = []
        for p in range(PAGES_PER_CHUNK):
            page = pt_ref[bb, c * PAGES_PER_CHUNK + p]
            for cache, buf in caches:
                copies.append(pltpu.make_async_copy(
                    cache.at[layer, page],
                    buf.at[slot, pl.ds(p * PAGE_SIZE, PAGE_SIZE), :],
                    sem.at[slot]))
        return copies

    @pl.when(b == 0)
    def _():
        for cp in chunk_copies(b, 0, 0):
            cp.start()

    ql = ql_ref[0]
    qr = qr_ref[0]
    iq = iq_ref[0]
    qb = qb_ref[0]
    w_col = iw_ref[0] * (IDX_SCALE * IDX_W_SCALE)

    def index_scores(ik):
        d = _nt(iq, ik)
        r = jnp.maximum(d, 0.0) * w_col
        return jnp.sum(r.reshape(N_IDX, T_PAD, r.shape[-1]), axis=0)

    m = jnp.full((ROWS_A, 1), -jnp.inf, F32)
    l = jnp.zeros((ROWS_A, 1), F32)
    acc = jnp.zeros((ROWS_A, KV_LORA), F32)

    def mla_update(s, kc, m, l, acc):
        m_new = jnp.maximum(m, jnp.max(s, axis=-1, keepdims=True))
        alpha = jnp.exp(m - m_new)
        p = jnp.exp(s - m_new)
        l = alpha * l + jnp.sum(p, axis=-1, keepdims=True)
        acc = alpha * acc + jnp.dot(p.astype(BF16), kc, preferred_element_type=F32)
        return m_new, l, acc

    for c in range(N_CHUNK):
        slot = c % 2
        if c + 1 < N_CHUNK:
            for cp in chunk_copies(b, c + 1, 1 - slot):
                cp.start()
        else:
            @pl.when(b + 1 < DEC_BATCH)
            def _():
                for cp in chunk_copies(b + 1, 0, 1 - slot):
                    cp.start()
        for cp in chunk_copies(b, c, slot):
            cp.wait()

        kc = ckv_buf[slot].astype(BF16)
        kr = kr_buf[slot].astype(BF16)
        s = (_nt(ql, kc) + _nt(qr, kr)) * MLA_SCALE
        m, l, acc = mla_update(s, kc, m, l, acc)
        i_all[:, c * CHUNK:(c + 1) * CHUNK] = index_scores(ik_buf[slot].astype(BF16))
        sb_all[:, c * CHUNK:(c + 1) * CHUNK] = _nt(qb, k_buf[slot].astype(BF16)) * DSA_SCALE
        v_all[c * CHUNK:(c + 1) * CHUNK, :] = v_buf[slot].astype(BF16)

    def pad_keys(x):
        return jnp.concatenate([x, jnp.zeros((LANES - T_PAD, x.shape[-1]), F32)], axis=0).astype(BF16)

    t_a = lax.broadcasted_iota(I32, (ROWS_A, LANES), 0) & (T_PAD - 1)
    j_a = lax.broadcasted_iota(I32, (ROWS_A, LANES), 1)
    kc = pad_keys(ckvn_ref[0])
    s = (_nt(ql, kc) + _nt(qr, pad_keys(krn_ref[0]))) * MLA_SCALE
    s = jnp.where(j_a <= t_a, s, NEG)
    m, l, acc = mla_update(s, kc, m, l, acc)
    ol_ref[0] = acc / l

    t_i = lax.broadcasted_iota(I32, (T_PAD, LANES), 0)
    j_i = lax.broadcasted_iota(I32, (T_PAD, LANES), 1)
    i_all[:, PAST_LEN:] = jnp.where(j_i <= t_i, index_scores(pad_keys(ikn_ref[0])), -jnp.inf)
    sb_all[:, PAST_LEN:] = _nt(qb, pad_keys(kbn_ref[0])) * DSA_SCALE
    v_all[PAST_LEN:, :] = pad_keys(vbn_ref[0])

    sel = _topk_select(i_all[...], K_TOP_SAMPLE, IDX_BITS_SAMPLE)
    t_s = lax.broadcasted_iota(I32, (T_PAD, S_ALL), 0)
    k_s = lax.broadcasted_iota(I32, (T_PAD, S_ALL), 1)
    causal = k_s <= PAST_LEN + t_s
    bias = jnp.where(causal, (sel - 1.0) * -NEG, NEG)
    sb = sb_all[...].reshape(H_B, T_PAD, S_ALL) + bias[None]
    mb = jnp.max(sb, axis=-1, keepdims=True)
    pb = jnp.exp(sb - mb)
    lb = jnp.sum(pb, axis=-1, keepdims=True).reshape(ROWS_A, 1)
    ob = jnp.dot(pb.reshape(ROWS_A, S_ALL).astype(BF16), v_all[...], preferred_element_type=F32)
    ob_ref[0] = ob / lb


def _sample_attn(page_table, layer, ql, qr, iq, iw, qb, ckvn, krn, ikn, kbn, vbn,
                 c_ckv, c_kr, c_ik, c_k, c_v):
    per_seq = lambda r, w: pl.BlockSpec((1, r, w), lambda b, pt, l: (b, 0, 0))
    any_spec = pl.BlockSpec(memory_space=pl.ANY)
    grid_spec = pltpu.PrefetchScalarGridSpec(
        num_scalar_prefetch=2,
        grid=(DEC_BATCH,),
        in_specs=[
            per_seq(ROWS_A, KV_LORA), per_seq(ROWS_A, ROPE_DIM), per_seq(ROWS_I, D_IDX),
            per_seq(ROWS_I, 1), per_seq(ROWS_A, HEAD_DIM_B),
            per_seq(T_PAD, KV_LORA), per_seq(T_PAD, ROPE_DIM), per_seq(T_PAD, D_IDX),
            per_seq(T_PAD, HEAD_DIM_B), per_seq(T_PAD, HEAD_DIM_B),
            any_spec, any_spec, any_spec, any_spec, any_spec,
        ],
        out_specs=(per_seq(ROWS_A, KV_LORA), per_seq(ROWS_A, HEAD_DIM_B)),
        scratch_shapes=[
            pltpu.VMEM((2, CHUNK, KV_LORA), F32),
            pltpu.VMEM((2, CHUNK, ROPE_DIM), F32),
            pltpu.VMEM((2, CHUNK, D_IDX), F32),
            pltpu.VMEM((2, CHUNK, HEAD_DIM_B), F32),
            pltpu.VMEM((2, CHUNK, HEAD_DIM_B), F32),
            pltpu.SemaphoreType.DMA((2,)),
            pltpu.VMEM((T_PAD, S_ALL), F32),
            pltpu.VMEM((ROWS_A, S_ALL), F32),
            pltpu.VMEM((S_ALL, HEAD_DIM_B), BF16),
        ],
    )
    return pl.pallas_call(
        _sample_kernel,
        out_shape=(jax.ShapeDtypeStruct((DEC_BATCH, ROWS_A, KV_LORA), F32),
                   jax.ShapeDtypeStruct((DEC_BATCH, ROWS_A, HEAD_DIM_B), F32)),
        grid_spec=grid_spec,
        compiler_params=pltpu.CompilerParams(
            dimension_semantics=("arbitrary",), vmem_limit_bytes=VMEM_LIMIT),
        name="sample_attn",
    )(page_table, layer, ql, qr, iq, iw, qb, ckvn, krn, ikn, kbn, vbn, c_ckv, c_kr, c_ik, c_k, c_v)


TM_OUT = 256
N_PT_OUT = N_PROMPT // TM_OUT
N_ST_OUT = N_SAMPLE // TM_OUT


def _silu(x):
    return x * (1.0 / (1.0 + jnp.exp(-x)))


def _combine_kernel(olp_ref, ols_ref, obp_ref, obs_ref, ga_ref, gb_ref, x_ref, wuv_ref, wout_ref,
                    gfin_ref, y_ref, mix_sc, *, final):
    is_prompt = pl.program_id(0) < N_PT_OUT

    def fill(ol_ref, ob_ref):
        for h in range(H_A):
            oa = jnp.dot(ol_ref[:, h * KV_LORA:(h + 1) * KV_LORA], wuv_ref[h],
                         preferred_element_type=F32)
            gate = _silu(ga_ref[:, h * V_DIM:(h + 1) * V_DIM])
            mix_sc[:, h * V_DIM:(h + 1) * V_DIM] = (oa * gate).astype(BF16)
        mix_sc[:, H_A * V_DIM:] = (ob_ref[...] * _silu(gb_ref[...])).astype(BF16)

    @pl.when(is_prompt)
    def _():
        fill(olp_ref, obp_ref)

    @pl.when(jnp.logical_not(is_prompt))
    def _():
        fill(ols_ref, obs_ref)

    y = x_ref[...] + jnp.dot(mix_sc[...], wout_ref[...], preferred_element_type=F32)
    if final:
        y = _rms(y, gfin_ref[...])
    y_ref[...] = y


def _combine(olp, ols, obp, obs, z, x, wuv, wout, gfin, final):
    p_idx = lambda i: (jnp.minimum(i, N_PT_OUT - 1), 0)
    s_idx = lambda i: (jnp.clip(i - N_PT_OUT, 0, N_ST_OUT - 1), 0)
    return pl.pallas_call(
        functools.partial(_combine_kernel, final=final),
        out_shape=jax.ShapeDtypeStruct((N_ROWS, D_MODEL), F32),
        grid=(N_ROWS // TM_OUT,),
        in_specs=[
            pl.BlockSpec((TM_OUT, H_A * KV_LORA), p_idx),
            pl.BlockSpec((TM_OUT, H_A * KV_LORA), s_idx),
            pl.BlockSpec((TM_OUT, H_B * HEAD_DIM_B), p_idx),
            pl.BlockSpec((TM_OUT, H_B * HEAD_DIM_B), s_idx),
            pl.BlockSpec((TM_OUT, 1024), lambda i: (i, OFF_GA // 1024)),
            pl.BlockSpec((TM_OUT, 1024), lambda i: (i, OFF_GB // 1024)),
            pl.BlockSpec((TM_OUT, D_MODEL), lambda i: (i, 0)),
            pl.BlockSpec((H_A, KV_LORA, V_DIM), lambda i: (0, 0, 0)),
            pl.BlockSpec((H_A * V_DIM + H_B * HEAD_DIM_B, D_MODEL), lambda i: (0, 0)),
            pl.BlockSpec((1, D_MODEL), lambda i: (0, 0)),
        ],
        out_specs=pl.BlockSpec((TM_OUT, D_MODEL), lambda i: (i, 0)),
        scratch_shapes=[pltpu.VMEM((TM_OUT, H_A * V_DIM + H_B * HEAD_DIM_B), BF16)],
        compiler_params=pltpu.CompilerParams(
            dimension_semantics=("parallel",), vmem_limit_bytes=VMEM_LIMIT),
        name="combine_final" if final else "combine",
    )(olp, ols, obp, obs, z, z, x, wuv, wout, gfin)


def _rope_tables(pos):
    posf = pos.astype(F32)

    def cs(half, theta):
        freqs = jnp.power(jnp.float32(theta), -jnp.arange(half, dtype=F32) / half)
        ang = posf[:, None] * freqs[None, :]
        return jnp.cos(ang), jnp.sin(ang)

    n = pos.shape[0]
    zeros = lambda w: jnp.zeros((n, w), F32)
    ones = lambda w: jnp.ones((n, w), F32)
    cat = lambda xs: jnp.concatenate(xs, axis=-1)

    cm, sm = cs(ROPE_DIM // 2, THETA_MLA)
    c_m64, sa_m64, sb_m64 = cat([cm, cm]), cat([-sm, zeros(32)]), cat([zeros(32), sm])
    cb, sb = cs(ROT_B // 2, THETA_PARTIAL)
    c_b = cat([cb, cb, ones(96)])
    sa_b = cat([-sb, zeros(112)])
    sb_b = cat([zeros(16), sb, zeros(96)])
    cx, sx = cs(ROT_IDX // 2, THETA_PARTIAL)
    c_x64 = cat([cx, cx, ones(48)])
    sa_x64 = cat([-sx, zeros(56)])
    sb_x64 = cat([zeros(8), sx, zeros(48)])
    twice = lambda t: cat([t, t])
    tabs = [
        twice(c_m64), twice(sa_m64), twice(sb_m64),
        c_b, sa_b, sb_b,
        twice(c_x64), twice(sa_x64), twice(sb_x64),
        cat([c_m64, c_x64]),
        cat([sa_m64, zeros(64)]), cat([sb_m64, zeros(64)]),
        cat([zeros(64), sa_x64]), cat([zeros(64), sb_x64]),
    ]
    return jnp.stack(tabs)


def _pack_w_in(w):
    cq, ckv, kr, ga, qb, kb, vb, iq, ik, iw, gb = jnp.split(
        w, [512, 768, 832, 1856, 2880, 3008, 3136, 4160, 4224, 4240], axis=-1)
    pad = jnp.zeros((w.shape[0], 112), w.dtype)
    return jnp.concatenate([cq, ckv, kr, ik, iw, pad, qb, iq, ga, gb, kb, vb], axis=-1).astype(BF16)


def _pack_w_uq(w):
    w = w.reshape(Q_LORA, H_A, NOPE_DIM + ROPE_DIM)
    nope = w[:, :, :NOPE_DIM].reshape(Q_LORA, H_A * NOPE_DIM)
    rope = w[:, :, NOPE_DIM:].reshape(Q_LORA, H_A * ROPE_DIM)
    return jnp.concatenate([nope, rope], axis=-1).astype(BF16)


def _sample_rows(a, nh, lo, hi):
    a = a[N_PROMPT // QB:, :, :, lo:hi]
    a = a.transpose(1, 0, 2, 3).reshape(nh, DEC_BATCH, DEC_SEQ, hi - lo)
    a = jnp.pad(a.transpose(1, 0, 2, 3), ((0, 0), (0, 0), (0, T_PAD - DEC_SEQ), (0, 0)))
    return a.reshape(DEC_BATCH, nh * T_PAD, hi - lo)


def _new_rows(a, lo, hi):
    a = a[N_PROMPT:, lo:hi].reshape(DEC_BATCH, DEC_SEQ, hi - lo)
    return jnp.pad(a, ((0, 0), (0, T_PAD - DEC_SEQ), (0, 0)))


def _from_seq_rows(a, nh):
    w = a.shape[-1]
    a = a.reshape(DEC_BATCH, nh, T_PAD, w)[:, :, :DEC_SEQ]
    return a.transpose(0, 2, 1, 3).reshape(N_SAMPLE, nh * w)


def kernel(x_prompt, x_sample, cache_mla_ckv, cache_mla_krope, cache_dsa_k, cache_dsa_v, cache_dsa_idxk,
           page_table, g_attn, w_in, g_q, w_uq, g_kv, w_uk, w_uv, w_out, g_final):
    x = jnp.concatenate([x_prompt.reshape(N_PROMPT, D_MODEL), x_sample.reshape(N_SAMPLE, D_MODEL)])
    pos = jnp.concatenate([jnp.tile(jnp.arange(SEQ), BATCH),
                           jnp.tile(PAST_LEN + jnp.arange(DEC_SEQ), DEC_BATCH)])
    tabs = _rope_tables(pos)
    gfin = g_final.reshape(1, D_MODEL)

    rows = []
    for l in range(DEPTH):
        final = l == DEPTH - 1
        z = _in_proj(x, g_attn[l].reshape(1, D_MODEL), _pack_w_in(w_in[l]))
        (ckv, krik, kb, vb, ckvb, krikb, kbb, vbb, ql, qr, qb, iq, iw) = _post_proj(
            z, tabs, g_q[l].reshape(1, Q_LORA), g_kv[l].reshape(1, KV_LORA),
            _pack_w_uq(w_uq[l]), w_uk[l].astype(BF16))
        rows.append((ckv, krik[:, :ROPE_DIM], kb, vb, krik[:, ROPE_DIM:]))

        ol_p = _mla_prompt(ql, qr, ckvb, krikb)
        ob_p = _dsa_prompt(iq, iw, qb, krikb, kbb, vbb)

        iw_s = iw[N_PROMPT:, :N_IDX].reshape(DEC_BATCH, DEC_SEQ, N_IDX).transpose(0, 2, 1)
        iw_s = jnp.pad(iw_s, ((0, 0), (0, 0), (0, T_PAD - DEC_SEQ))).reshape(DEC_BATCH, ROWS_I, 1)
        ol_s, ob_s = _sample_attn(
            page_table, jnp.full((1,), l, I32),
            _sample_rows(ql, H_A, 0, KV_LORA), _sample_rows(qr, H_A, 0, ROPE_DIM),
            _sample_rows(iq, N_IDX, D_IDX, LANES), iw_s, _sample_rows(qb, H_B, 0, HEAD_DIM_B),
            _new_rows(ckv, 0, KV_LORA), _new_rows(krik, 0, ROPE_DIM), _new_rows(krik, ROPE_DIM, LANES),
            _new_rows(kb, 0, HEAD_DIM_B), _new_rows(vb, 0, HEAD_DIM_B),
            cache_mla_ckv, cache_mla_krope, cache_dsa_idxk, cache_dsa_k, cache_dsa_v)
        x = _combine(ol_p, _from_seq_rows(ol_s, H_A).astype(BF16), ob_p, _from_seq_rows(ob_s, H_B),
                     z, x, w_uv[l].astype(BF16), w_out[l].astype(BF16), gfin, final)

    y_prompt = x[:N_PROMPT].reshape(BATCH, SEQ, D_MODEL)
    y_sample = x[N_PROMPT:].reshape(DEC_BATCH, DEC_SEQ, D_MODEL)

    def stack(k, lo, hi, shape):
        return jnp.stack([r[k][lo:hi].reshape(shape + (r[k].shape[-1],)) for r in rows])

    p_out = [stack(k, 0, N_PROMPT, (BATCH, SEQ)) for k in range(5)]
    s_out = [stack(k, N_PROMPT, N_ROWS, (DEC_BATCH, DEC_SEQ)) for k in range(5)]
    return (y_prompt, y_sample, *p_out, *s_out)
```

```python
import functools

import jax
import jax.numpy as jnp
from jax import lax
from jax.experimental import pallas as pl
from jax.experimental.pallas import tpu as pltpu

F32 = jnp.float32
BF16 = jnp.bfloat16
I32 = jnp.int32

D_MODEL = 2048
BATCH = 4
SEQ = 2048
DEPTH = 4
DEC_BATCH = 128
DEC_SEQ = 4
PAST_LEN = 8192
PAGE_SIZE = 128
H_A = 8
V_DIM = 128
NOPE_DIM = 128
ROPE_DIM = 64
KV_LORA = 256
Q_LORA = 512
H_B = 8
HEAD_DIM_B = 128
ROT_B = HEAD_DIM_B // 4
N_IDX = 16
D_IDX = 64
ROT_IDX = D_IDX // 4
TOPK_MAX = 256
THETA_PARTIAL = 500000.0
THETA_MLA = 10000.0
MLA_SCALE = (NOPE_DIM + ROPE_DIM) ** -0.5
DSA_SCALE = HEAD_DIM_B ** -0.5
IDX_SCALE = D_IDX ** -0.5
IDX_W_SCALE = N_IDX ** -0.5
EPS = 1e-6
NEG = -1e30

LANES = 128
N_PROMPT = BATCH * SEQ
N_SAMPLE = DEC_BATCH * DEC_SEQ
N_ROWS = N_PROMPT + N_SAMPLE
QB = 128
N_QBLK = N_ROWS // QB
T_PAD = 8

OFF_A = 0
OFF_QB = 1024
OFF_IQ = 2048
OFF_GA = 3072
OFF_GB = 4096
OFF_KV = 5120
Z_WIDTH = 5376

INT_MIN = -(2 ** 31)
VMEM_LIMIT = 52 * 1024 * 1024


def _nt(a, b):
    return lax.dot_general(a, b, (((1,), (1,)), ((), ())), preferred_element_type=F32)


def _rms(x, g):
    var = jnp.mean(x * x, axis=-1, keepdims=True)
    return (x * lax.rsqrt(var + EPS)) * g


def _rope3(x, c, sa, sb, half):
    return x * c + pltpu.roll(x, LANES - half, 1) * sa + pltpu.roll(x, half, 1) * sb


TM_IN = 512
TN_IN = 768


def _in_proj_kernel(x_ref, g_ref, w_ref, z_ref, h_sc):
    @pl.when(pl.program_id(1) == 0)
    def _():
        h_sc[...] = _rms(x_ref[...], g_ref[...]).astype(BF16)

    z_ref[...] = jnp.dot(h_sc[...], w_ref[...], preferred_element_type=F32)


def _in_proj(x, g, w):
    return pl.pallas_call(
        _in_proj_kernel,
        out_shape=jax.ShapeDtypeStruct((N_ROWS, Z_WIDTH), F32),
        grid=(N_ROWS // TM_IN, Z_WIDTH // TN_IN),
        in_specs=[
            pl.BlockSpec((TM_IN, D_MODEL), lambda i, j: (i, 0)),
            pl.BlockSpec((1, D_MODEL), lambda i, j: (0, 0)),
            pl.BlockSpec((D_MODEL, TN_IN), lambda i, j: (0, j)),
        ],
        out_specs=pl.BlockSpec((TM_IN, TN_IN), lambda i, j: (i, j)),
        scratch_shapes=[pltpu.VMEM((TM_IN, D_MODEL), BF16)],
        compiler_params=pltpu.CompilerParams(
            dimension_semantics=("parallel", "arbitrary"), vmem_limit_bytes=VMEM_LIMIT),
        name="in_proj",
    )(x, g, w)


TM_POST = 256
R_POST = TM_POST // QB
T_CM, T_SAM, T_SBM, T_CB, T_SAB, T_SBB, T_CX, T_SAX, T_SBX, T_CK, T_SAMK, T_SBMK, T_SAXK, T_SBXK = range(14)
N_TAB = 14


def _post_kernel(za_ref, zqb_ref, ziq_ref, zkv_ref, tab_ref, gq_ref, gkv_ref, wuq_ref, wuk_ref,
                 ckv_o, krik_o, kb_o, vb_o, ckvb_o, krikb_o, kbb_o, vbb_o,
                 ql_o, qr_o, qb_o, iq_o, iw_o):
    lane = lax.broadcasted_iota(I32, (TM_POST, LANES), 1)
    lo_half = lane < 64

    cqn = _rms(za_ref[:, 0:Q_LORA], gq_ref[...]).astype(BF16)
    q = jnp.dot(cqn, wuq_ref[...], preferred_element_type=F32)
    for h in range(H_A):
        qn = q[:, h * NOPE_DIM:(h + 1) * NOPE_DIM].astype(BF16)
        qlat = jnp.dot(qn, wuk_ref[h], preferred_element_type=F32).astype(BF16)
        for r in range(R_POST):
            ql_o[r, h] = qlat[r * QB:(r + 1) * QB]
    c_m, sa_m, sb_m = tab_ref[T_CM], tab_ref[T_SAM], tab_ref[T_SBM]
    for j in range(H_A * ROPE_DIM // LANES):
        x = q[:, H_A * NOPE_DIM + j * LANES:H_A * NOPE_DIM + (j + 1) * LANES]
        y = _rope3(x, c_m, sa_m, sb_m, ROPE_DIM // 2)
        y_lo = jnp.where(lo_half, y, 0.0).astype(BF16)
        y_hi = jnp.where(lo_half, pltpu.roll(y, 64, 1), 0.0).astype(BF16)
        for r in range(R_POST):
            qr_o[r, 2 * j] = y_lo[r * QB:(r + 1) * QB]
            qr_o[r, 2 * j + 1] = y_hi[r * QB:(r + 1) * QB]

    ckv = _rms(za_ref[:, Q_LORA:Q_LORA + KV_LORA], gkv_ref[...])
    ckv_o[...] = ckv
    ckvb_o[...] = ckv.astype(BF16)
    x = za_ref[:, 768:896]
    krik = (x * tab_ref[T_CK]
            + pltpu.roll(x, LANES - ROPE_DIM // 2, 1) * tab_ref[T_SAMK]
            + pltpu.roll(x, ROPE_DIM // 2, 1) * tab_ref[T_SBMK]
            + pltpu.roll(x, LANES - ROT_IDX // 2, 1) * tab_ref[T_SAXK]
            + pltpu.roll(x, ROT_IDX // 2, 1) * tab_ref[T_SBXK])
    krik_o[...] = krik
    krikb_o[...] = krik.astype(BF16)
    iw_o[...] = za_ref[:, 896:1024]

    c_b, sa_b, sb_b = tab_ref[T_CB], tab_ref[T_SAB], tab_ref[T_SBB]
    for h in range(H_B):
        y = _rope3(zqb_ref[:, h * LANES:(h + 1) * LANES], c_b, sa_b, sb_b, ROT_B // 2).astype(BF16)
        for r in range(R_POST):
            qb_o[r, h] = y[r * QB:(r + 1) * QB]
    kb = _rope3(zkv_ref[:, 0:LANES], c_b, sa_b, sb_b, ROT_B // 2)
    kb_o[...] = kb
    kbb_o[...] = kb.astype(BF16)
    vb = zkv_ref[:, LANES:2 * LANES]
    vb_o[...] = vb
    vbb_o[...] = vb.astype(BF16)

    c_x, sa_x, sb_x = tab_ref[T_CX], tab_ref[T_SAX], tab_ref[T_SBX]
    for j in range(N_IDX * D_IDX // LANES):
        y = _rope3(ziq_ref[:, j * LANES:(j + 1) * LANES], c_x, sa_x, sb_x, ROT_IDX // 2)
        y_even = jnp.where(lo_half, 0.0, pltpu.roll(y, 64, 1)).astype(BF16)
        y_odd = jnp.where(lo_half, 0.0, y).astype(BF16)
        for r in range(R_POST):
            iq_o[r, 2 * j] = y_even[r * QB:(r + 1) * QB]
            iq_o[r, 2 * j + 1] = y_odd[r * QB:(r + 1) * QB]


def _post_proj(z, tabs, gq, gkv, wuq, wuk):
    row = lambda w: pl.BlockSpec((TM_POST, w), lambda i: (i, 0))
    blk = lambda nh: pl.BlockSpec((R_POST, nh, QB, LANES), lambda i: (i, 0, 0, 0))
    out_shape = (
        jax.ShapeDtypeStruct((N_ROWS, KV_LORA), F32),
        jax.ShapeDtypeStruct((N_ROWS, LANES), F32),
        jax.ShapeDtypeStruct((N_ROWS, LANES), F32),
        jax.ShapeDtypeStruct((N_ROWS, LANES), F32),
        jax.ShapeDtypeStruct((N_ROWS, KV_LORA), BF16),
        jax.ShapeDtypeStruct((N_ROWS, LANES), BF16),
        jax.ShapeDtypeStruct((N_ROWS, LANES), BF16),
        jax.ShapeDtypeStruct((N_ROWS, LANES), BF16),
        jax.ShapeDtypeStruct((N_QBLK, H_A, QB, KV_LORA), BF16),
        jax.ShapeDtypeStruct((N_QBLK, H_A, QB, LANES), BF16),
        jax.ShapeDtypeStruct((N_QBLK, H_B, QB, LANES), BF16),
        jax.ShapeDtypeStruct((N_QBLK, N_IDX, QB, LANES), BF16),
        jax.ShapeDtypeStruct((N_ROWS, LANES), F32),
    )
    out_specs = (
        row(KV_LORA), row(LANES), row(LANES), row(LANES),
        row(KV_LORA), row(LANES), row(LANES), row(LANES),
        pl.BlockSpec((R_POST, H_A, QB, KV_LORA), lambda i: (i, 0, 0, 0)),
        blk(H_A), blk(H_B), blk(N_IDX), row(LANES),
    )
    return pl.pallas_call(
        _post_kernel,
        out_shape=out_shape,
        grid=(N_ROWS // TM_POST,),
        in_specs=[
            pl.BlockSpec((TM_POST, 1024), lambda i: (i, OFF_A // 1024)),
            pl.BlockSpec((TM_POST, 1024), lambda i: (i, OFF_QB // 1024)),
            pl.BlockSpec((TM_POST, 1024), lambda i: (i, OFF_IQ // 1024)),
            pl.BlockSpec((TM_POST, 256), lambda i: (i, OFF_KV // 256)),
            pl.BlockSpec((N_TAB, TM_POST, LANES), lambda i: (0, i, 0)),
            pl.BlockSpec((1, Q_LORA), lambda i: (0, 0)),
            pl.BlockSpec((1, KV_LORA), lambda i: (0, 0)),
            pl.BlockSpec((Q_LORA, H_A * (NOPE_DIM + ROPE_DIM)), lambda i: (0, 0)),
            pl.BlockSpec((H_A, NOPE_DIM, KV_LORA), lambda i: (0, 0, 0)),
        ],
        out_specs=out_specs,
        compiler_params=pltpu.CompilerParams(
            dimension_semantics=("parallel",), vmem_limit_bytes=VMEM_LIMIT),
        name="post_proj",
    )(z, z, z, z, tabs, gq, gkv, wuq, wuk)


N_QB_SEQ = SEQ // QB
SPAN = 512
QB_PER_SPAN = SPAN // QB
N_SPANS = SEQ // SPAN


def _for_causal_span(i, body):
    for r in range(N_SPANS):
        pl.when(i // QB_PER_SPAN == r)(functools.partial(body, SPAN * (r + 1)))


def _mla_prompt_kernel(ql_ref, qr_ref, ckv_ref, krik_ref, o_ref):
    i = pl.program_id(1)

    def attend(n_keys):
        ql = ql_ref[0].reshape(H_A * QB, KV_LORA)
        qr = qr_ref[0].reshape(H_A * QB, LANES)
        kc = ckv_ref[0:n_keys, :]
        kr = krik_ref[0:n_keys, :]
        s = (_nt(ql, kc) + _nt(qr, kr)) * MLA_SCALE
        lo = n_keys - SPAN
        t = lax.broadcasted_iota(I32, (H_A * QB, SPAN), 0) & (QB - 1)
        kpos = lo + lax.broadcasted_iota(I32, (H_A * QB, SPAN), 1)
        tail = jnp.where(kpos <= i * QB + t, s[:, lo:], NEG)
        s = tail if lo == 0 else jnp.concatenate([s[:, :lo], tail], axis=-1)
        p = jnp.exp(s - jnp.max(s, axis=-1, keepdims=True))
        l = jnp.sum(p, axis=-1, keepdims=True)
        o = jnp.dot(p.astype(BF16), kc, preferred_element_type=F32) / l
        for h in range(H_A):
            o_ref[:, h * KV_LORA:(h + 1) * KV_LORA] = o[h * QB:(h + 1) * QB].astype(BF16)

    _for_causal_span(i, attend)


def _mla_prompt(ql, qr, ckvb, krikb):
    return pl.pallas_call(
        _mla_prompt_kernel,
        out_shape=jax.ShapeDtypeStruct((N_PROMPT, H_A * KV_LORA), BF16),
        grid=(BATCH, N_QB_SEQ),
        in_specs=[
            pl.BlockSpec((1, H_A, QB, KV_LORA), lambda b, i: (b * N_QB_SEQ + i, 0, 0, 0)),
            pl.BlockSpec((1, H_A, QB, LANES), lambda b, i: (b * N_QB_SEQ + i, 0, 0, 0)),
            pl.BlockSpec((SEQ, KV_LORA), lambda b, i: (b, 0)),
            pl.BlockSpec((SEQ, LANES), lambda b, i: (b, 0)),
        ],
        out_specs=pl.BlockSpec((QB, H_A * KV_LORA), lambda b, i: (b * N_QB_SEQ + i, 0)),
        compiler_params=pltpu.CompilerParams(
            dimension_semantics=("parallel", "arbitrary"), vmem_limit_bytes=VMEM_LIMIT),
        name="mla_prompt",
    )(ql, qr, ckvb, krikb)


def _count(mask):
    return jnp.sum(jnp.where(mask, 1.0, 0.0), axis=-1, keepdims=True)


def _topk_select(scores, k):
    rows, n = scores.shape
    idx_bits = n.bit_length()
    bits = lax.bitcast_convert_type(scores, I32)
    key = bits ^ ((bits >> 31) & 0x7FFFFFFF)
    kf = jnp.float32(k)

    base = jnp.full((rows, 1), INT_MIN, I32)
    cand = jnp.zeros((rows, 1), I32)
    base = jnp.where(_count(key >= cand) >= kf, cand, base)

    def value_bit(b, base):
        cand = base | (jnp.int32(1) << (30 - b))
        return jnp.where(_count(key >= cand) >= kf, cand, base)

    thr = lax.fori_loop(0, 31, value_bit, base)
    gt = key > thr
    eq = key == thr
    idx = lax.broadcasted_iota(I32, scores.shape, 1)

    def tie_cut():
        need = kf - _count(gt)

        def index_bit(b, cut):
            cand = cut | (jnp.int32(1) << (idx_bits - 1 - b))
            below = jnp.where(eq, jnp.where(idx < cand, 1.0, 0.0), 0.0)
            return jnp.where(jnp.sum(below, axis=-1, keepdims=True) <= need, cand, cut)

        return lax.fori_loop(0, idx_bits, index_bit, jnp.zeros((rows, 1), I32))

    has_tie = jnp.max(_count(key >= thr)) > kf
    cut = lax.cond(has_tie, tie_cut, lambda: jnp.full((rows, 1), 2 ** idx_bits, I32))
    return jnp.where(gt, 1.0, jnp.where(eq, jnp.where(idx < cut, 1.0, 0.0), 0.0))


K_TOP_PROMPT = min(TOPK_MAX, SEQ // 4)


def _dsa_prompt_kernel(iq_ref, iw_ref, qb_ref, krik_ref, kb_ref, vb_ref, o_ref, i_sc, bias_sc):
    i = pl.program_id(1)

    def attend(n_keys):
        krik = krik_ref[0:n_keys, :]
        w_all = iw_ref[...] * (IDX_SCALE * IDX_W_SCALE)
        for h in range(N_IDX):
            term = jnp.maximum(_nt(iq_ref[0, h], krik), 0.0) * w_all[:, h:h + 1]
            if h == 0:
                i_sc[:, 0:n_keys] = term
            else:
                i_sc[:, 0:n_keys] += term
        qpos = i * QB + lax.broadcasted_iota(I32, (QB, n_keys), 0)
        kpos = lax.broadcasted_iota(I32, (QB, n_keys), 1)
        causal = kpos <= qpos
        sel = _topk_select(jnp.where(causal, i_sc[:, 0:n_keys], -jnp.inf), K_TOP_PROMPT)
        bias_sc[:, 0:n_keys] = jnp.where(causal, (sel - 1.0) * -NEG, NEG)

        kb = kb_ref[0:n_keys, :]
        vb = vb_ref[0:n_keys, :]
        for h in range(H_B):
            s = _nt(qb_ref[0, h], kb) * DSA_SCALE + bias_sc[:, 0:n_keys]
            p = jnp.exp(s - jnp.max(s, axis=-1, keepdims=True))
            l = jnp.sum(p, axis=-1, keepdims=True)
            o = jnp.dot(p.astype(BF16), vb, preferred_element_type=F32) / l
            o_ref[:, h * HEAD_DIM_B:(h + 1) * HEAD_DIM_B] = o

    _for_causal_span(i, attend)


def _dsa_prompt(iq, iw, qb, krikb, kbb, vbb):
    blk = lambda nh: pl.BlockSpec((1, nh, QB, LANES), lambda b, i: (b * N_QB_SEQ + i, 0, 0, 0))
    seq = pl.BlockSpec((SEQ, LANES), lambda b, i: (b, 0))
    return pl.pallas_call(
        _dsa_prompt_kernel,
        out_shape=jax.ShapeDtypeStruct((N_PROMPT, H_B * HEAD_DIM_B), F32),
        grid=(BATCH, N_QB_SEQ),
        in_specs=[
            blk(N_IDX),
            pl.BlockSpec((QB, LANES), lambda b, i: (b * N_QB_SEQ + i, 0)),
            blk(H_B), seq, seq, seq,
        ],
        out_specs=pl.BlockSpec((QB, H_B * HEAD_DIM_B), lambda b, i: (b * N_QB_SEQ + i, 0)),
        scratch_shapes=[pltpu.VMEM((QB, SEQ), F32), pltpu.VMEM((QB, SEQ), F32)],
        compiler_params=pltpu.CompilerParams(
            dimension_semantics=("parallel", "arbitrary"), vmem_limit_bytes=VMEM_LIMIT),
        name="dsa_prompt",
    )(iq, iw, qb, krikb, kbb, vbb)


PAGES_PER_CHUNK = 16
CHUNK = PAGES_PER_CHUNK * PAGE_SIZE
N_CHUNK = PAST_LEN // CHUNK
N_PAGES = PAST_LEN // PAGE_SIZE
S_ALL = PAST_LEN + LANES
K_TOP_SAMPLE = min(TOPK_MAX, (PAST_LEN + DEC_SEQ) // 4)
ROWS_A = H_A * T_PAD
ROWS_I = N_IDX * T_PAD
SEQ_PER_TOPK = 16


def _pad_keys(x):
    return jnp.concatenate([x, jnp.zeros((LANES - T_PAD, x.shape[-1]), F32)], axis=0).astype(BF16)


def _sample_index_kernel(pt_ref, l_ref, iq_ref, iw_ref, ikn_ref, c_ik, i_ref, ik_buf, sem):
    b = pl.program_id(0)
    layer = l_ref[0]
    slot = b & 1

    def page_copies(bb, s):
        return [pltpu.make_async_copy(c_ik.at[layer, pt_ref[bb, p]],
                                      ik_buf.at[s, :, pl.ds(p * PAGE_SIZE, PAGE_SIZE)],
                                      sem.at[s]) for p in range(N_PAGES)]

    @pl.when(b == 0)
    def _():
        for cp in page_copies(b, slot):
            cp.start()

    @pl.when(b + 1 < DEC_BATCH)
    def _():
        for cp in page_copies(b + 1, 1 - slot):
            cp.start()

    for cp in page_copies(b, slot):
        cp.wait()

    iq = iq_ref[0]
    w_col = iw_ref[0] * (IDX_SCALE * IDX_W_SCALE)

    def head_sum(d):
        r = jnp.maximum(d, 0.0) * w_col
        return jnp.sum(r.reshape(N_IDX, T_PAD, r.shape[-1]), axis=0)

    for c in range(N_CHUNK):
        ik_t = ik_buf[slot, :, c * CHUNK:(c + 1) * CHUNK].astype(BF16)
        i_ref[0, :, c * CHUNK:(c + 1) * CHUNK] = head_sum(
            jnp.dot(iq, ik_t, preferred_element_type=F32))
    t_i = lax.broadcasted_iota(I32, (T_PAD, LANES), 0)
    j_i = lax.broadcasted_iota(I32, (T_PAD, LANES), 1)
    i_ref[0, :, PAST_LEN:] = jnp.where(j_i <= t_i, head_sum(_nt(iq, _pad_keys(ikn_ref[0]))), -jnp.inf)


def _sample_index(page_table, layer, iq, iw, ikn, c_ik):
    per_seq = lambda r, w: pl.BlockSpec((1, r, w), lambda b, pt, l: (b, 0, 0))
    grid_spec = pltpu.PrefetchScalarGridSpec(
        num_scalar_prefetch=2,
        grid=(DEC_BATCH,),
        in_specs=[per_seq(ROWS_I, D_IDX), per_seq(ROWS_I, 1), per_seq(T_PAD, D_IDX),
                  pl.BlockSpec(memory_space=pl.ANY)],
        out_specs=per_seq(T_PAD, S_ALL),
        scratch_shapes=[pltpu.VMEM((2, D_IDX, PAST_LEN), F32), pltpu.SemaphoreType.DMA((2,))],
    )
    return pl.pallas_call(
        _sample_index_kernel,
        out_shape=jax.ShapeDtypeStruct((DEC_BATCH, T_PAD, S_ALL), F32),
        grid_spec=grid_spec,
        compiler_params=pltpu.CompilerParams(
            dimension_semantics=("arbitrary",), vmem_limit_bytes=VMEM_LIMIT),
        name="sample_index",
    )(page_table, layer, iq, iw, ikn, c_ik)


def _sample_topk_kernel(i_ref, bias_ref):
    rows = SEQ_PER_TOPK * T_PAD
    sel = _topk_select(i_ref[...].reshape(rows, S_ALL), K_TOP_SAMPLE)
    t = lax.broadcasted_iota(I32, (rows, S_ALL), 0) & (T_PAD - 1)
    kpos = lax.broadcasted_iota(I32, (rows, S_ALL), 1)
    bias = jnp.where(kpos <= PAST_LEN + t, (sel - 1.0) * -NEG, NEG)
    bias_ref[...] = bias.reshape(SEQ_PER_TOPK, T_PAD, S_ALL)


def _sample_topk(scores):
    spec = pl.BlockSpec((SEQ_PER_TOPK, T_PAD, S_ALL), lambda g: (g, 0, 0))
    return pl.pallas_call(
        _sample_topk_kernel,
        out_shape=jax.ShapeDtypeStruct((DEC_BATCH, T_PAD, S_ALL), F32),
        grid=(DEC_BATCH // SEQ_PER_TOPK,),
        in_specs=[spec],
        out_specs=spec,
        compiler_params=pltpu.CompilerParams(
            dimension_semantics=("parallel",), vmem_limit_bytes=VMEM_LIMIT),
        name="sample_topk",
    )(scores)


def _sample_attn_kernel(pt_ref, l_ref,
                        ql_ref, qr_ref, qb_ref, bias_ref, ckvn_ref, krn_ref, kbn_ref, vbn_ref,
                        c_ckv, c_kr, c_k, c_v,
                        ol_ref, ob_ref,
                        ckv_buf, kr_buf, k_buf, v_buf, sem):
    b = pl.program_id(0)
    layer = l_ref[0]

    def chunk_copies(bb, c, slot):
        copies = []
        for p in range(PAGES_PER_CHUNK):
            page = pt_ref[bb, c * PAGES_PER_CHUNK + p]
            keys = pl.ds(p * PAGE_SIZE, PAGE_SIZE)
            copies += [
                pltpu.make_async_copy(c_ckv.at[layer, page], ckv_buf.at[slot, keys, :], sem.at[slot]),
                pltpu.make_async_copy(c_kr.at[layer, page], kr_buf.at[slot, :, keys], sem.at[slot]),
                pltpu.make_async_copy(c_k.at[layer, page], k_buf.at[slot, keys, :], sem.at[slot]),
                pltpu.make_async_copy(c_v.at[layer, page], v_buf.at[slot, keys, :], sem.at[slot]),
            ]
        return copies

    @pl.when(b == 0)
    def _():
        for cp in chunk_copies(b, 0, 0):
            cp.start()

    ql = ql_ref[0]
    qr = qr_ref[0]
    qb = qb_ref[0]

    def update(s, v, state):
        m, l, acc = state
        m_new = jnp.maximum(m, jnp.max(s, axis=-1, keepdims=True))
        alpha = jnp.exp(m - m_new)
        p = jnp.exp(s - m_new)
        l = alpha * l + jnp.sum(p, axis=-1, keepdims=True)
        acc = alpha * acc + jnp.dot(p.astype(BF16), v, preferred_element_type=F32)
        return m_new, l, acc

    def masked(sb, bias):
        n = sb.shape[-1]
        return (sb.reshape(H_B, T_PAD, n) + bias[None]).reshape(ROWS_A, n)

    init = lambda d: (jnp.full((ROWS_A, 1), -jnp.inf, F32), jnp.zeros((ROWS_A, 1), F32),
                      jnp.zeros((ROWS_A, d), F32))
    st_a = init(KV_LORA)
    st_b = init(HEAD_DIM_B)

    for c in range(N_CHUNK):
        slot = c % 2
        if c + 1 < N_CHUNK:
            for cp in chunk_copies(b, c + 1, 1 - slot):
                cp.start()
        else:
            @pl.when(b + 1 < DEC_BATCH)
            def _():
                for cp in chunk_copies(b + 1, 0, 1 - slot):
                    cp.start()
        for cp in chunk_copies(b, c, slot):
            cp.wait()

        kc = ckv_buf[slot].astype(BF16)
        kr_t = kr_buf[slot].astype(BF16)
        s = (_nt(ql, kc) + jnp.dot(qr, kr_t, preferred_element_type=F32)) * MLA_SCALE
        st_a = update(s, kc, st_a)
        sb = _nt(qb, k_buf[slot].astype(BF16)) * DSA_SCALE
        st_b = update(masked(sb, bias_ref[0, :, c * CHUNK:(c + 1) * CHUNK]),
                      v_buf[slot].astype(BF16), st_b)

    t_a = lax.broadcasted_iota(I32, (ROWS_A, LANES), 0) & (T_PAD - 1)
    j_a = lax.broadcasted_iota(I32, (ROWS_A, LANES), 1)
    kc = _pad_keys(ckvn_ref[0])
    s = (_nt(ql, kc) + _nt(qr, _pad_keys(krn_ref[0]))) * MLA_SCALE
    m, l, acc = update(jnp.where(j_a <= t_a, s, NEG), kc, st_a)
    ol_ref[0] = acc / l

    sb = _nt(qb, _pad_keys(kbn_ref[0])) * DSA_SCALE
    m, l, acc = update(masked(sb, bias_ref[0, :, PAST_LEN:]), _pad_keys(vbn_ref[0]), st_b)
    ob_ref[0] = acc / l


def _sample_attn(page_table, layer, ql, qr, qb, bias, ckvn, krn, kbn, vbn, c_ckv, c_kr, c_k, c_v):
    per_seq = lambda r, w: pl.BlockSpec((1, r, w), lambda b, pt, l: (b, 0, 0))
    any_spec = pl.BlockSpec(memory_space=pl.ANY)
    grid_spec = pltpu.PrefetchScalarGridSpec(
        num_scalar_prefetch=2,
        grid=(DEC_BATCH,),
        in_specs=[
            per_seq(ROWS_A, KV_LORA), per_seq(ROWS_A, ROPE_DIM), per_seq(ROWS_A, HEAD_DIM_B),
            per_seq(T_PAD, S_ALL),
            per_seq(T_PAD, KV_LORA), per_seq(T_PAD, ROPE_DIM),
            per_seq(T_PAD, HEAD_DIM_B), per_seq(T_PAD, HEAD_DIM_B),
            any_spec, any_spec, any_spec, any_spec,
        ],
        out_specs=(per_seq(ROWS_A, KV_LORA), per_seq(ROWS_A, HEAD_DIM_B)),
        scratch_shapes=[
            pltpu.VMEM((2, CHUNK, KV_LORA), F32),
            pltpu.VMEM((2, ROPE_DIM, CHUNK), F32),
            pltpu.VMEM((2, CHUNK, HEAD_DIM_B), F32),
            pltpu.VMEM((2, CHUNK, HEAD_DIM_B), F32),
            pltpu.SemaphoreType.DMA((2,)),
        ],
    )
    return pl.pallas_call(
        _sample_attn_kernel,
        out_shape=(jax.ShapeDtypeStruct((DEC_BATCH, ROWS_A, KV_LORA), F32),
                   jax.ShapeDtypeStruct((DEC_BATCH, ROWS_A, HEAD_DIM_B), F32)),
        grid_spec=grid_spec,
        compiler_params=pltpu.CompilerParams(
            dimension_semantics=("arbitrary",), vmem_limit_bytes=VMEM_LIMIT),
        name="sample_attn",
    )(page_table, layer, ql, qr, qb, bias, ckvn, krn, kbn, vbn, c_ckv, c_kr, c_k, c_v)


TM_OUT = 256
N_PT_OUT = N_PROMPT // TM_OUT
N_ST_OUT = N_SAMPLE // TM_OUT


def _silu(x):
    return x * (1.0 / (1.0 + jnp.exp(-x)))


def _combine_kernel(olp_ref, ols_ref, obp_ref, obs_ref, ga_ref, gb_ref, x_ref, wuv_ref, wout_ref,
                    gfin_ref, y_ref, mix_sc, *, final):
    is_prompt = pl.program_id(0) < N_PT_OUT

    def fill(ol_ref, ob_ref):
        for h in range(H_A):
            oa = jnp.dot(ol_ref[:, h * KV_LORA:(h + 1) * KV_LORA], wuv_ref[h],
                         preferred_element_type=F32)
            gate = _silu(ga_ref[:, h * V_DIM:(h + 1) * V_DIM])
            mix_sc[:, h * V_DIM:(h + 1) * V_DIM] = (oa * gate).astype(BF16)
        mix_sc[:, H_A * V_DIM:] = (ob_ref[...] * _silu(gb_ref[...])).astype(BF16)

    @pl.when(is_prompt)
    def _():
        fill(olp_ref, obp_ref)

    @pl.when(jnp.logical_not(is_prompt))
    def _():
        fill(ols_ref, obs_ref)

    y = x_ref[...] + jnp.dot(mix_sc[...], wout_ref[...], preferred_element_type=F32)
    if final:
        y = _rms(y, gfin_ref[...])
    y_ref[...] = y


def _combine(olp, ols, obp, obs, z, x, wuv, wout, gfin, final):
    p_idx = lambda i: (jnp.minimum(i, N_PT_OUT - 1), 0)
    s_idx = lambda i: (jnp.clip(i - N_PT_OUT, 0, N_ST_OUT - 1), 0)
    return pl.pallas_call(
        functools.partial(_combine_kernel, final=final),
        out_shape=jax.ShapeDtypeStruct((N_ROWS, D_MODEL), F32),
        grid=(N_ROWS // TM_OUT,),
        in_specs=[
            pl.BlockSpec((TM_OUT, H_A * KV_LORA), p_idx),
            pl.BlockSpec((TM_OUT, H_A * KV_LORA), s_idx),
            pl.BlockSpec((TM_OUT, H_B * HEAD_DIM_B), p_idx),
            pl.BlockSpec((TM_OUT, H_B * HEAD_DIM_B), s_idx),
            pl.BlockSpec((TM_OUT, 1024), lambda i: (i, OFF_GA // 1024)),
            pl.BlockSpec((TM_OUT, 1024), lambda i: (i, OFF_GB // 1024)),
            pl.BlockSpec((TM_OUT, D_MODEL), lambda i: (i, 0)),
            pl.BlockSpec((H_A, KV_LORA, V_DIM), lambda i: (0, 0, 0)),
            pl.BlockSpec((H_A * V_DIM + H_B * HEAD_DIM_B, D_MODEL), lambda i: (0, 0)),
            pl.BlockSpec((1, D_MODEL), lambda i: (0, 0)),
        ],
        out_specs=pl.BlockSpec((TM_OUT, D_MODEL), lambda i: (i, 0)),
        scratch_shapes=[pltpu.VMEM((TM_OUT, H_A * V_DIM + H_B * HEAD_DIM_B), BF16)],
        compiler_params=pltpu.CompilerParams(
            dimension_semantics=("parallel",), vmem_limit_bytes=VMEM_LIMIT),
        name="combine_final" if final else "combine",
    )(olp, ols, obp, obs, z, z, x, wuv, wout, gfin)


def _rope_tables(pos):
    posf = pos.astype(F32)

    def cs(half, theta):
        freqs = jnp.power(jnp.float32(theta), -jnp.arange(half, dtype=F32) / half)
        ang = posf[:, None] * freqs[None, :]
        return jnp.cos(ang), jnp.sin(ang)

    n = pos.shape[0]
    zeros = lambda w: jnp.zeros((n, w), F32)
    ones = lambda w: jnp.ones((n, w), F32)
    cat = lambda xs: jnp.concatenate(xs, axis=-1)

    cm, sm = cs(ROPE_DIM // 2, THETA_MLA)
    c_m64, sa_m64, sb_m64 = cat([cm, cm]), cat([-sm, zeros(32)]), cat([zeros(32), sm])
    cb, sb = cs(ROT_B // 2, THETA_PARTIAL)
    c_b = cat([cb, cb, ones(96)])
    sa_b = cat([-sb, zeros(112)])
    sb_b = cat([zeros(16), sb, zeros(96)])
    cx, sx = cs(ROT_IDX // 2, THETA_PARTIAL)
    c_x64 = cat([cx, cx, ones(48)])
    sa_x64 = cat([-sx, zeros(56)])
    sb_x64 = cat([zeros(8), sx, zeros(48)])
    twice = lambda t: cat([t, t])
    tabs = [
        twice(c_m64), twice(sa_m64), twice(sb_m64),
        c_b, sa_b, sb_b,
        twice(c_x64), twice(sa_x64), twice(sb_x64),
        cat([c_m64, c_x64]),
        cat([sa_m64, zeros(64)]), cat([sb_m64, zeros(64)]),
        cat([zeros(64), sa_x64]), cat([zeros(64), sb_x64]),
    ]
    return jnp.stack(tabs)


def _pack_w_in(w):
    cq, ckv, kr, ga, qb, kb, vb, iq, ik, iw, gb = jnp.split(
        w, [512, 768, 832, 1856, 2880, 3008, 3136, 4160, 4224, 4240], axis=-1)
    pad = jnp.zeros((w.shape[0], 112), w.dtype)
    return jnp.concatenate([cq, ckv, kr, ik, iw, pad, qb, iq, ga, gb, kb, vb], axis=-1).astype(BF16)


def _pack_w_uq(w):
    w = w.reshape(Q_LORA, H_A, NOPE_DIM + ROPE_DIM)
    nope = w[:, :, :NOPE_DIM].reshape(Q_LORA, H_A * NOPE_DIM)
    rope = w[:, :, NOPE_DIM:].reshape(Q_LORA, H_A * ROPE_DIM)
    return jnp.concatenate([nope, rope], axis=-1).astype(BF16)


def _sample_rows(a, nh, lo, hi):
    a = a[N_PROMPT // QB:, :, :, lo:hi]
    a = a.transpose(1, 0, 2, 3).reshape(nh, DEC_BATCH, DEC_SEQ, hi - lo)
    a = jnp.pad(a.transpose(1, 0, 2, 3), ((0, 0), (0, 0), (0, T_PAD - DEC_SEQ), (0, 0)))
    return a.reshape(DEC_BATCH, nh * T_PAD, hi - lo)


def _new_rows(a, lo, hi):
    a = a[N_PROMPT:, lo:hi].reshape(DEC_BATCH, DEC_SEQ, hi - lo)
    return jnp.pad(a, ((0, 0), (0, T_PAD - DEC_SEQ), (0, 0)))


def _from_seq_rows(a, nh):
    w = a.shape[-1]
    a = a.reshape(DEC_BATCH, nh, T_PAD, w)[:, :, :DEC_SEQ]
    return a.transpose(0, 2, 1, 3).reshape(N_SAMPLE, nh * w)


def kernel(x_prompt, x_sample, cache_mla_ckv, cache_mla_krope, cache_dsa_k, cache_dsa_v, cache_dsa_idxk,
           page_table, g_attn, w_in, g_q, w_uq, g_kv, w_uk, w_uv, w_out, g_final):
    x = jnp.concatenate([x_prompt.reshape(N_PROMPT, D_MODEL), x_sample.reshape(N_SAMPLE, D_MODEL)])
    pos = jnp.concatenate([jnp.tile(jnp.arange(SEQ), BATCH),
                           jnp.tile(PAST_LEN + jnp.arange(DEC_SEQ), DEC_BATCH)])
    tabs = _rope_tables(pos)
    gfin = g_final.reshape(1, D_MODEL)
    cache_kr_t = jnp.swapaxes(cache_mla_krope, 2, 3)
    cache_ik_t = jnp.swapaxes(cache_dsa_idxk, 2, 3)

    rows = []
    for l in range(DEPTH):
        final = l == DEPTH - 1
        layer = jnp.full((1,), l, I32)
        z = _in_proj(x, g_attn[l].reshape(1, D_MODEL), _pack_w_in(w_in[l]))
        (ckv, krik, kb, vb, ckvb, krikb, kbb, vbb, ql, qr, qb, iq, iw) = _post_proj(
            z, tabs, g_q[l].reshape(1, Q_LORA), g_kv[l].reshape(1, KV_LORA),
            _pack_w_uq(w_uq[l]), w_uk[l].astype(BF16))
        rows.append((ckv, krik[:, :ROPE_DIM], kb, vb, krik[:, ROPE_DIM:]))

        ol_p = _mla_prompt(ql, qr, ckvb, krikb)
        ob_p = _dsa_prompt(iq, iw, qb, krikb, kbb, vbb)

        iw_s = iw[N_PROMPT:, :N_IDX].reshape(DEC_BATCH, DEC_SEQ, N_IDX).transpose(0, 2, 1)
        iw_s = jnp.pad(iw_s, ((0, 0), (0, 0), (0, T_PAD - DEC_SEQ))).reshape(DEC_BATCH, ROWS_I, 1)
        scores = _sample_index(page_table, layer, _sample_rows(iq, N_IDX, D_IDX, LANES), iw_s,
                               _new_rows(krik, ROPE_DIM, LANES), cache_ik_t)
        ol_s, ob_s = _sample_attn(
            page_table, layer,
            _sample_rows(ql, H_A, 0, KV_LORA), _sample_rows(qr, H_A, 0, ROPE_DIM),
            _sample_rows(qb, H_B, 0, HEAD_DIM_B), _sample_topk(scores),
            _new_rows(ckv, 0, KV_LORA), _new_rows(krik, 0, ROPE_DIM),
            _new_rows(kb, 0, HEAD_DIM_B), _new_rows(vb, 0, HEAD_DIM_B),
            cache_mla_ckv, cache_kr_t, cache_dsa_k, cache_dsa_v)
        x = _combine(ol_p, _from_seq_rows(ol_s, H_A).astype(BF16), ob_p, _from_seq_rows(ob_s, H_B),
                     z, x, w_uv[l].astype(BF16), w_out[l].astype(BF16), gfin, final)

    y_prompt = x[:N_PROMPT].reshape(BATCH, SEQ, D_MODEL)
    y_sample = x[N_PROMPT:].reshape(DEC_BATCH, DEC_SEQ, D_MODEL)

    def stack(k, lo, hi, shape):
        return jnp.stack([r[k][lo:hi].reshape(shape + (r[k].shape[-1],)) for r in rows])

    p_out = [stack(k, 0, N_PROMPT, (BATCH, SEQ)) for k in range(5)]
    s_out = [stack(k, N_PROMPT, N_ROWS, (DEC_BATCH, DEC_SEQ)) for k in range(5)]
    return (y_prompt, y_sample, *p_out, *s_out)
```

```python
import functools

import jax
import jax.numpy as jnp
from jax import lax
from jax.experimental import pallas as pl
from jax.experimental.pallas import tpu as pltpu

F32 = jnp.float32
BF16 = jnp.bfloat16
I32 = jnp.int32

D_MODEL = 2048
BATCH = 4
SEQ = 2048
DEPTH = 4
DEC_BATCH = 128
DEC_SEQ = 4
PAST_LEN = 8192
PAGE_SIZE = 128
H_A = 8
V_DIM = 128
NOPE_DIM = 128
ROPE_DIM = 64
KV_LORA = 256
Q_LORA = 512
H_B = 8
HEAD_DIM_B = 128
ROT_B = HEAD_DIM_B // 4
N_IDX = 16
D_IDX = 64
ROT_IDX = D_IDX // 4
TOPK_MAX = 256
THETA_PARTIAL = 500000.0
THETA_MLA = 10000.0
MLA_SCALE = (NOPE_DIM + ROPE_DIM) ** -0.5
DSA_SCALE = HEAD_DIM_B ** -0.5
IDX_SCALE = D_IDX ** -0.5
IDX_W_SCALE = N_IDX ** -0.5
EPS = 1e-6
NEG = -1e30

LANES = 128
N_PROMPT = BATCH * SEQ
N_SAMPLE = DEC_BATCH * DEC_SEQ
N_ROWS = N_PROMPT + N_SAMPLE
QB = 128
N_QBLK = N_ROWS // QB
T_PAD = 8

OFF_A = 0
OFF_QB = 1024
OFF_IQ = 2048
OFF_GA = 3072
OFF_GB = 4096
OFF_KV = 5120
Z_WIDTH = 5376

INT_MIN = -(2 ** 31)
VMEM_LIMIT = 52 * 1024 * 1024


def _nt(a, b):
    return lax.dot_general(a, b, (((1,), (1,)), ((), ())), preferred_element_type=F32)


def _rms(x, g):
    var = jnp.mean(x * x, axis=-1, keepdims=True)
    return (x * lax.rsqrt(var + EPS)) * g


def _rope3(x, c, sa, sb, half):
    return x * c + pltpu.roll(x, LANES - half, 1) * sa + pltpu.roll(x, half, 1) * sb


TM_IN = 512
TN_IN = 1792


def _in_proj_kernel(x_ref, g_ref, w_ref, z_ref, h_sc):
    @pl.when(pl.program_id(1) == 0)
    def _():
        h_sc[...] = _rms(x_ref[...], g_ref[...]).astype(BF16)

    z_ref[...] = jnp.dot(h_sc[...], w_ref[...], preferred_element_type=F32)


def _in_proj(x, g, w):
    return pl.pallas_call(
        _in_proj_kernel,
        out_shape=jax.ShapeDtypeStruct((N_ROWS, Z_WIDTH), F32),
        grid=(N_ROWS // TM_IN, Z_WIDTH // TN_IN),
        in_specs=[
            pl.BlockSpec((TM_IN, D_MODEL), lambda i, j: (i, 0)),
            pl.BlockSpec((1, D_MODEL), lambda i, j: (0, 0)),
            pl.BlockSpec((D_MODEL, TN_IN), lambda i, j: (0, j)),
        ],
        out_specs=pl.BlockSpec((TM_IN, TN_IN), lambda i, j: (i, j)),
        scratch_shapes=[pltpu.VMEM((TM_IN, D_MODEL), BF16)],
        compiler_params=pltpu.CompilerParams(
            dimension_semantics=("parallel", "arbitrary"), vmem_limit_bytes=VMEM_LIMIT),
        name="in_proj",
    )(x, g, w)


TM_POST = 256
R_POST = TM_POST // QB
T_CM, T_SAM, T_SBM, T_CB, T_SAB, T_SBB, T_CX, T_SAX, T_SBX, T_CK, T_SAMK, T_SBMK, T_SAXK, T_SBXK = range(14)
N_TAB = 14


def _post_kernel(za_ref, zqb_ref, ziq_ref, zkv_ref, tab_ref, gq_ref, gkv_ref, wuq_ref, wuk_ref,
                 ckv_o, krik_o, kb_o, vb_o, ckvb_o, krikb_o, kbb_o, vbb_o,
                 ql_o, qr_o, qb_o, iq_o, iw_o):
    lane = lax.broadcasted_iota(I32, (TM_POST, LANES), 1)
    lo_half = lane < 64

    cqn = _rms(za_ref[:, 0:Q_LORA], gq_ref[...]).astype(BF16)
    q = jnp.dot(cqn, wuq_ref[...], preferred_element_type=F32)
    for h in range(H_A):
        qn = q[:, h * NOPE_DIM:(h + 1) * NOPE_DIM].astype(BF16)
        qlat = jnp.dot(qn, wuk_ref[h], preferred_element_type=F32).astype(BF16)
        for r in range(R_POST):
            ql_o[r, h] = qlat[r * QB:(r + 1) * QB]
    c_m, sa_m, sb_m = tab_ref[T_CM], tab_ref[T_SAM], tab_ref[T_SBM]
    for j in range(H_A * ROPE_DIM // LANES):
        x = q[:, H_A * NOPE_DIM + j * LANES:H_A * NOPE_DIM + (j + 1) * LANES]
        y = _rope3(x, c_m, sa_m, sb_m, ROPE_DIM // 2)
        y_lo = jnp.where(lo_half, y, 0.0).astype(BF16)
        y_hi = jnp.where(lo_half, pltpu.roll(y, 64, 1), 0.0).astype(BF16)
        for r in range(R_POST):
            qr_o[r, 2 * j] = y_lo[r * QB:(r + 1) * QB]
            qr_o[r, 2 * j + 1] = y_hi[r * QB:(r + 1) * QB]

    ckv = _rms(za_ref[:, Q_LORA:Q_LORA + KV_LORA], gkv_ref[...])
    ckv_o[...] = ckv
    ckvb_o[...] = ckv.astype(BF16)
    x = za_ref[:, 768:896]
    krik = (x * tab_ref[T_CK]
            + pltpu.roll(x, LANES - ROPE_DIM // 2, 1) * tab_ref[T_SAMK]
            + pltpu.roll(x, ROPE_DIM // 2, 1) * tab_ref[T_SBMK]
            + pltpu.roll(x, LANES - ROT_IDX // 2, 1) * tab_ref[T_SAXK]
            + pltpu.roll(x, ROT_IDX // 2, 1) * tab_ref[T_SBXK])
    krik_o[...] = krik
    krikb_o[...] = krik.astype(BF16)
    iw_o[...] = za_ref[:, 896:1024]

    c_b, sa_b, sb_b = tab_ref[T_CB], tab_ref[T_SAB], tab_ref[T_SBB]
    for h in range(H_B):
        y = _rope3(zqb_ref[:, h * LANES:(h + 1) * LANES], c_b, sa_b, sb_b, ROT_B // 2).astype(BF16)
        for r in range(R_POST):
            qb_o[r, h] = y[r * QB:(r + 1) * QB]
    kb = _rope3(zkv_ref[:, 0:LANES], c_b, sa_b, sb_b, ROT_B // 2)
    kb_o[...] = kb
    kbb_o[...] = kb.astype(BF16)
    vb = zkv_ref[:, LANES:2 * LANES]
    vb_o[...] = vb
    vbb_o[...] = vb.astype(BF16)

    c_x, sa_x, sb_x = tab_ref[T_CX], tab_ref[T_SAX], tab_ref[T_SBX]
    for j in range(N_IDX * D_IDX // LANES):
        y = _rope3(ziq_ref[:, j * LANES:(j + 1) * LANES], c_x, sa_x, sb_x, ROT_IDX // 2)
        y_even = jnp.where(lo_half, 0.0, pltpu.roll(y, 64, 1)).astype(BF16)
        y_odd = jnp.where(lo_half, 0.0, y).astype(BF16)
        for r in range(R_POST):
            iq_o[r, 2 * j] = y_even[r * QB:(r + 1) * QB]
            iq_o[r, 2 * j + 1] = y_odd[r * QB:(r + 1) * QB]


def _post_proj(z, tabs, gq, gkv, wuq, wuk):
    row = lambda w: pl.BlockSpec((TM_POST, w), lambda i: (i, 0))
    blk = lambda nh: pl.BlockSpec((R_POST, nh, QB, LANES), lambda i: (i, 0, 0, 0))
    out_shape = (
        jax.ShapeDtypeStruct((N_ROWS, KV_LORA), F32),
        jax.ShapeDtypeStruct((N_ROWS, LANES), F32),
        jax.ShapeDtypeStruct((N_ROWS, LANES), F32),
        jax.ShapeDtypeStruct((N_ROWS, LANES), F32),
        jax.ShapeDtypeStruct((N_ROWS, KV_LORA), BF16),
        jax.ShapeDtypeStruct((N_ROWS, LANES), BF16),
        jax.ShapeDtypeStruct((N_ROWS, LANES), BF16),
        jax.ShapeDtypeStruct((N_ROWS, LANES), BF16),
        jax.ShapeDtypeStruct((N_QBLK, H_A, QB, KV_LORA), BF16),
        jax.ShapeDtypeStruct((N_QBLK, H_A, QB, LANES), BF16),
        jax.ShapeDtypeStruct((N_QBLK, H_B, QB, LANES), BF16),
        jax.ShapeDtypeStruct((N_QBLK, N_IDX, QB, LANES), BF16),
        jax.ShapeDtypeStruct((N_ROWS, LANES), F32),
    )
    out_specs = (
        row(KV_LORA), row(LANES), row(LANES), row(LANES),
        row(KV_LORA), row(LANES), row(LANES), row(LANES),
        pl.BlockSpec((R_POST, H_A, QB, KV_LORA), lambda i: (i, 0, 0, 0)),
        blk(H_A), blk(H_B), blk(N_IDX), row(LANES),
    )
    return pl.pallas_call(
        _post_kernel,
        out_shape=out_shape,
        grid=(N_ROWS // TM_POST,),
        in_specs=[
            pl.BlockSpec((TM_POST, 1024), lambda i: (i, OFF_A // 1024)),
            pl.BlockSpec((TM_POST, 1024), lambda i: (i, OFF_QB // 1024)),
            pl.BlockSpec((TM_POST, 1024), lambda i: (i, OFF_IQ // 1024)),
            pl.BlockSpec((TM_POST, 256), lambda i: (i, OFF_KV // 256)),
            pl.BlockSpec((N_TAB, TM_POST, LANES), lambda i: (0, i, 0)),
            pl.BlockSpec((1, Q_LORA), lambda i: (0, 0)),
            pl.BlockSpec((1, KV_LORA), lambda i: (0, 0)),
            pl.BlockSpec((Q_LORA, H_A * (NOPE_DIM + ROPE_DIM)), lambda i: (0, 0)),
            pl.BlockSpec((H_A, NOPE_DIM, KV_LORA), lambda i: (0, 0, 0)),
        ],
        out_specs=out_specs,
        compiler_params=pltpu.CompilerParams(
            dimension_semantics=("parallel",), vmem_limit_bytes=VMEM_LIMIT),
        name="post_proj",
    )(z, z, z, z, tabs, gq, gkv, wuq, wuk)


N_QB_SEQ = SEQ // QB
SPAN = 512
QB_PER_SPAN = SPAN // QB
N_SPANS = SEQ // SPAN


def _for_causal_span(i, body, blocks_per_step=1):
    for r in range(N_SPANS):
        pl.when(i // (QB_PER_SPAN // blocks_per_step) == r)(functools.partial(body, SPAN * (r + 1)))


def _mla_prompt_kernel(ql_ref, qr_ref, ckv_ref, krik_ref, o_ref):
    i = pl.program_id(1)

    def attend(n_keys):
        ql = ql_ref[0].reshape(H_A * QB, KV_LORA)
        qr = qr_ref[0].reshape(H_A * QB, LANES)
        kc = ckv_ref[0:n_keys, :]
        kr = krik_ref[0:n_keys, :]
        s = (_nt(ql, kc) + _nt(qr, kr)) * MLA_SCALE
        lo = n_keys - SPAN
        t = lax.broadcasted_iota(I32, (H_A * QB, SPAN), 0) & (QB - 1)
        kpos = lo + lax.broadcasted_iota(I32, (H_A * QB, SPAN), 1)
        tail = jnp.where(kpos <= i * QB + t, s[:, lo:], NEG)
        s = tail if lo == 0 else jnp.concatenate([s[:, :lo], tail], axis=-1)
        p = jnp.exp(s - jnp.max(s, axis=-1, keepdims=True))
        l = jnp.sum(p, axis=-1, keepdims=True)
        o = jnp.dot(p.astype(BF16), kc, preferred_element_type=F32) / l
        for h in range(H_A):
            o_ref[:, h * KV_LORA:(h + 1) * KV_LORA] = o[h * QB:(h + 1) * QB].astype(BF16)

    _for_causal_span(i, attend)


def _mla_prompt(ql, qr, ckvb, krikb):
    return pl.pallas_call(
        _mla_prompt_kernel,
        out_shape=jax.ShapeDtypeStruct((N_PROMPT, H_A * KV_LORA), BF16),
        grid=(BATCH, N_QB_SEQ),
        in_specs=[
            pl.BlockSpec((1, H_A, QB, KV_LORA), lambda b, i: (b * N_QB_SEQ + i, 0, 0, 0)),
            pl.BlockSpec((1, H_A, QB, LANES), lambda b, i: (b * N_QB_SEQ + i, 0, 0, 0)),
            pl.BlockSpec((SEQ, KV_LORA), lambda b, i: (b, 0)),
            pl.BlockSpec((SEQ, LANES), lambda b, i: (b, 0)),
        ],
        out_specs=pl.BlockSpec((QB, H_A * KV_LORA), lambda b, i: (b * N_QB_SEQ + i, 0)),
        compiler_params=pltpu.CompilerParams(
            dimension_semantics=("parallel", "arbitrary"), vmem_limit_bytes=VMEM_LIMIT),
        name="mla_prompt",
    )(ql, qr, ckvb, krikb)


def _count(mask):
    return jnp.sum(jnp.where(mask, 1.0, 0.0), axis=-1, keepdims=True)


def _topk_select(scores, k):
    rows, n = scores.shape
    idx_bits = n.bit_length()
    bits = lax.bitcast_convert_type(scores, I32)
    key = bits ^ ((bits >> 31) & 0x7FFFFFFF)
    kf = jnp.float32(k)

    base = jnp.full((rows, 1), INT_MIN, I32)
    cand = jnp.zeros((rows, 1), I32)
    base = jnp.where(_count(key >= cand) >= kf, cand, base)

    def value_bit(b, base):
        cand = base | (jnp.int32(1) << (30 - b))
        return jnp.where(_count(key >= cand) >= kf, cand, base)

    thr = lax.fori_loop(0, 31, value_bit, base)
    gt = key > thr
    eq = key == thr
    idx = lax.broadcasted_iota(I32, scores.shape, 1)

    def tie_cut():
        need = kf - _count(gt)

        def index_bit(b, cut):
            cand = cut | (jnp.int32(1) << (idx_bits - 1 - b))
            below = jnp.where(eq, jnp.where(idx < cand, 1.0, 0.0), 0.0)
            return jnp.where(jnp.sum(below, axis=-1, keepdims=True) <= need, cand, cut)

        return lax.fori_loop(0, idx_bits, index_bit, jnp.zeros((rows, 1), I32))

    has_tie = jnp.max(_count(key >= thr)) > kf
    cut = lax.cond(has_tie, tie_cut, lambda: jnp.full((rows, 1), 2 ** idx_bits, I32))
    return jnp.where(gt, 1.0, jnp.where(eq, jnp.where(idx < cut, 1.0, 0.0), 0.0))


K_TOP_PROMPT = min(TOPK_MAX, SEQ // 4)


QB_PER_DSA_STEP = 2
ROWS_DSA = QB_PER_DSA_STEP * QB
N_DSA_STEPS = N_QB_SEQ // QB_PER_DSA_STEP


def _dsa_prompt_kernel(iq_ref, iw_ref, qb_ref, krik_ref, kb_ref, vb_ref, o_ref, i_sc, bias_sc):
    i = pl.program_id(1)

    def attend(n_keys):
        krik = krik_ref[0:n_keys, :]
        w_all = iw_ref[...] * (IDX_SCALE * IDX_W_SCALE)
        for u in range(QB_PER_DSA_STEP):
            rows = slice(u * QB, (u + 1) * QB)
            for h in range(N_IDX):
                term = jnp.maximum(_nt(iq_ref[u, h], krik), 0.0) * w_all[rows, h:h + 1]
                if h == 0:
                    i_sc[rows, 0:n_keys] = term
                else:
                    i_sc[rows, 0:n_keys] += term
        qpos = i * ROWS_DSA + lax.broadcasted_iota(I32, (ROWS_DSA, n_keys), 0)
        kpos = lax.broadcasted_iota(I32, (ROWS_DSA, n_keys), 1)
        causal = kpos <= qpos
        sel = _topk_select(jnp.where(causal, i_sc[:, 0:n_keys], -jnp.inf), K_TOP_PROMPT)
        bias_sc[:, 0:n_keys] = jnp.where(causal, (sel - 1.0) * -NEG, NEG)

        kb = kb_ref[0:n_keys, :]
        vb = vb_ref[0:n_keys, :]
        for u in range(QB_PER_DSA_STEP):
            rows = slice(u * QB, (u + 1) * QB)
            for h in range(H_B):
                s = _nt(qb_ref[u, h], kb) * DSA_SCALE + bias_sc[rows, 0:n_keys]
                p = jnp.exp(s - jnp.max(s, axis=-1, keepdims=True))
                l = jnp.sum(p, axis=-1, keepdims=True)
                o = jnp.dot(p.astype(BF16), vb, preferred_element_type=F32) / l
                o_ref[rows, h * HEAD_DIM_B:(h + 1) * HEAD_DIM_B] = o

    _for_causal_span(i, attend, QB_PER_DSA_STEP)


def _dsa_prompt(iq, iw, qb, krikb, kbb, vbb):
    step = lambda b, i: b * N_DSA_STEPS + i
    blk = lambda nh: pl.BlockSpec((QB_PER_DSA_STEP, nh, QB, LANES), lambda b, i: (step(b, i), 0, 0, 0))
    seq = pl.BlockSpec((SEQ, LANES), lambda b, i: (b, 0))
    return pl.pallas_call(
        _dsa_prompt_kernel,
        out_shape=jax.ShapeDtypeStruct((N_PROMPT, H_B * HEAD_DIM_B), F32),
        grid=(BATCH, N_DSA_STEPS),
        in_specs=[
            blk(N_IDX),
            pl.BlockSpec((ROWS_DSA, LANES), lambda b, i: (step(b, i), 0)),
            blk(H_B), seq, seq, seq,
        ],
        out_specs=pl.BlockSpec((ROWS_DSA, H_B * HEAD_DIM_B), lambda b, i: (step(b, i), 0)),
        scratch_shapes=[pltpu.VMEM((ROWS_DSA, SEQ), F32), pltpu.VMEM((ROWS_DSA, SEQ), F32)],
        compiler_params=pltpu.CompilerParams(
            dimension_semantics=("parallel", "arbitrary"), vmem_limit_bytes=VMEM_LIMIT),
        name="dsa_prompt",
    )(iq, iw, qb, krikb, kbb, vbb)


PAGES_PER_CHUNK = 16
CHUNK = PAGES_PER_CHUNK * PAGE_SIZE
N_CHUNK = PAST_LEN // CHUNK
N_PAGES = PAST_LEN // PAGE_SIZE
S_ALL = PAST_LEN + LANES
K_TOP_SAMPLE = min(TOPK_MAX, (PAST_LEN + DEC_SEQ) // 4)
ROWS_A = H_A * T_PAD
ROWS_I = N_IDX * T_PAD
SEQ_PER_TOPK = 16


def _pad_keys(x):
    return jnp.concatenate([x, jnp.zeros((LANES - T_PAD, x.shape[-1]), F32)], axis=0).astype(BF16)


def _sample_index_kernel(pt_ref, l_ref, iq_ref, iw_ref, ikn_ref, c_ik, i_ref, ik_buf, sem):
    b = pl.program_id(0)
    layer = l_ref[0]
    slot = b & 1

    def page_copies(bb, s):
        return [pltpu.make_async_copy(c_ik.at[layer, pt_ref[bb, p]],
                                      ik_buf.at[s, :, pl.ds(p * PAGE_SIZE, PAGE_SIZE)],
                                      sem.at[s]) for p in range(N_PAGES)]

    @pl.when(b == 0)
    def _():
        for cp in page_copies(b, slot):
            cp.start()

    @pl.when(b + 1 < DEC_BATCH)
    def _():
        for cp in page_copies(b + 1, 1 - slot):
            cp.start()

    for cp in page_copies(b, slot):
        cp.wait()

    iq = iq_ref[0]
    w_col = iw_ref[0] * (IDX_SCALE * IDX_W_SCALE)

    def head_sum(d):
        r = jnp.maximum(d, 0.0) * w_col
        return jnp.sum(r.reshape(N_IDX, T_PAD, r.shape[-1]), axis=0)

    for c in range(N_CHUNK):
        ik_t = ik_buf[slot, :, c * CHUNK:(c + 1) * CHUNK].astype(BF16)
        i_ref[0, :, c * CHUNK:(c + 1) * CHUNK] = head_sum(
            jnp.dot(iq, ik_t, preferred_element_type=F32))
    t_i = lax.broadcasted_iota(I32, (T_PAD, LANES), 0)
    j_i = lax.broadcasted_iota(I32, (T_PAD, LANES), 1)
    i_ref[0, :, PAST_LEN:] = jnp.where(j_i <= t_i, head_sum(_nt(iq, _pad_keys(ikn_ref[0]))), -jnp.inf)


def _sample_index(page_table, layer, iq, iw, ikn, c_ik):
    per_seq = lambda r, w: pl.BlockSpec((1, r, w), lambda b, pt, l: (b, 0, 0))
    grid_spec = pltpu.PrefetchScalarGridSpec(
        num_scalar_prefetch=2,
        grid=(DEC_BATCH,),
        in_specs=[per_seq(ROWS_I, D_IDX), per_seq(ROWS_I, 1), per_seq(T_PAD, D_IDX),
                  pl.BlockSpec(memory_space=pl.ANY)],
        out_specs=per_seq(T_PAD, S_ALL),
        scratch_shapes=[pltpu.VMEM((2, D_IDX, PAST_LEN), F32), pltpu.SemaphoreType.DMA((2,))],
    )
    return pl.pallas_call(
        _sample_index_kernel,
        out_shape=jax.ShapeDtypeStruct((DEC_BATCH, T_PAD, S_ALL), F32),
        grid_spec=grid_spec,
        compiler_params=pltpu.CompilerParams(
            dimension_semantics=("arbitrary",), vmem_limit_bytes=VMEM_LIMIT),
        name="sample_index",
    )(page_table, layer, iq, iw, ikn, c_ik)


def _sample_topk_kernel(i_ref, bias_ref):
    pairs = SEQ_PER_TOPK // 2
    half = T_PAD // 2
    row = lax.broadcasted_iota(I32, (pairs, T_PAD, S_ALL), 1)
    merged = jnp.where(row >= half, pltpu.roll(i_ref[:, 1], half, 1), i_ref[:, 0])
    sel = _topk_select(merged.reshape(pairs * T_PAD, S_ALL), K_TOP_SAMPLE)
    kpos = lax.broadcasted_iota(I32, (pairs, T_PAD, S_ALL), 2)
    causal = kpos <= PAST_LEN + (row & (half - 1))
    bias = jnp.where(causal, (sel.reshape(pairs, T_PAD, S_ALL) - 1.0) * -NEG, NEG)
    bias_ref[:, 0] = bias
    bias_ref[:, 1] = pltpu.roll(bias, half, 1)


def _sample_topk(scores):
    pairs = SEQ_PER_TOPK // 2
    spec = pl.BlockSpec((pairs, 2, T_PAD, S_ALL), lambda g: (g, 0, 0, 0))
    bias = pl.pallas_call(
        _sample_topk_kernel,
        out_shape=jax.ShapeDtypeStruct((DEC_BATCH // 2, 2, T_PAD, S_ALL), F32),
        grid=(DEC_BATCH // SEQ_PER_TOPK,),
        in_specs=[spec],
        out_specs=spec,
        compiler_params=pltpu.CompilerParams(
            dimension_semantics=("parallel",), vmem_limit_bytes=VMEM_LIMIT),
        name="sample_topk",
    )(scores.reshape(DEC_BATCH // 2, 2, T_PAD, S_ALL))
    return bias.reshape(DEC_BATCH, T_PAD, S_ALL)


def _sample_attn_kernel(pt_ref, l_ref,
                        ql_ref, qr_ref, qb_ref, bias_ref, ckvn_ref, krn_ref, kbn_ref, vbn_ref,
                        c_ckv, c_kr, c_k, c_v,
                        ol_ref, ob_ref,
                        ckv_buf, kr_buf, k_buf, v_buf, sem):
    b = pl.program_id(0)
    layer = l_ref[0]

    def chunk_copies(bb, c, slot):
        copies = []
        for p in range(PAGES_PER_CHUNK):
            page = pt_ref[bb, c * PAGES_PER_CHUNK + p]
            keys = pl.ds(p * PAGE_SIZE, PAGE_SIZE)
            copies += [
                pltpu.make_async_copy(c_ckv.at[layer, page], ckv_buf.at[slot, keys, :], sem.at[slot]),
                pltpu.make_async_copy(c_kr.at[layer, page], kr_buf.at[slot, :, keys], sem.at[slot]),
                pltpu.make_async_copy(c_k.at[layer, page], k_buf.at[slot, keys, :], sem.at[slot]),
                pltpu.make_async_copy(c_v.at[layer, page], v_buf.at[slot, keys, :], sem.at[slot]),
            ]
        return copies

    @pl.when(b == 0)
    def _():
        for cp in chunk_copies(b, 0, 0):
            cp.start()

    ql = ql_ref[0]
    qr = qr_ref[0]
    qb = qb_ref[0]

    def update(s, v, state):
        m, l, acc = state
        m_new = jnp.maximum(m, jnp.max(s, axis=-1, keepdims=True))
        alpha = jnp.exp(m - m_new)
        p = jnp.exp(s - m_new)
        l = alpha * l + jnp.sum(p, axis=-1, keepdims=True)
        acc = alpha * acc + jnp.dot(p.astype(BF16), v, preferred_element_type=F32)
        return m_new, l, acc

    def masked(sb, bias):
        n = sb.shape[-1]
        return (sb.reshape(H_B, T_PAD, n) + bias[None]).reshape(ROWS_A, n)

    init = lambda d: (jnp.full((ROWS_A, 1), -jnp.inf, F32), jnp.zeros((ROWS_A, 1), F32),
                      jnp.zeros((ROWS_A, d), F32))
    st_a = init(KV_LORA)
    st_b = init(HEAD_DIM_B)

    for c in range(N_CHUNK):
        slot = c % 2
        if c + 1 < N_CHUNK:
            for cp in chunk_copies(b, c + 1, 1 - slot):
                cp.start()
        else:
            @pl.when(b + 1 < DEC_BATCH)
            def _():
                for cp in chunk_copies(b + 1, 0, 1 - slot):
                    cp.start()
        for cp in chunk_copies(b, c, slot):
            cp.wait()

        kc = ckv_buf[slot].astype(BF16)
        kr_t = kr_buf[slot].astype(BF16)
        s = (_nt(ql, kc) + jnp.dot(qr, kr_t, preferred_element_type=F32)) * MLA_SCALE
        st_a = update(s, kc, st_a)
        sb = _nt(qb, k_buf[slot].astype(BF16)) * DSA_SCALE
        st_b = update(masked(sb, bias_ref[0, :, c * CHUNK:(c + 1) * CHUNK]),
                      v_buf[slot].astype(BF16), st_b)

    t_a = lax.broadcasted_iota(I32, (ROWS_A, LANES), 0) & (T_PAD - 1)
    j_a = lax.broadcasted_iota(I32, (ROWS_A, LANES), 1)
    kc = _pad_keys(ckvn_ref[0])
    s = (_nt(ql, kc) + _nt(qr, _pad_keys(krn_ref[0]))) * MLA_SCALE
    m, l, acc = update(jnp.where(j_a <= t_a, s, NEG), kc, st_a)
    ol_ref[0] = acc / l

    sb = _nt(qb, _pad_keys(kbn_ref[0])) * DSA_SCALE
    m, l, acc = update(masked(sb, bias_ref[0, :, PAST_LEN:]), _pad_keys(vbn_ref[0]), st_b)
    ob_ref[0] = acc / l


def _sample_attn(page_table, layer, ql, qr, qb, bias, ckvn, krn, kbn, vbn, c_ckv, c_kr, c_k, c_v):
    per_seq = lambda r, w: pl.BlockSpec((1, r, w), lambda b, pt, l: (b, 0, 0))
    any_spec = pl.BlockSpec(memory_space=pl.ANY)
    grid_spec = pltpu.PrefetchScalarGridSpec(
        num_scalar_prefetch=2,
        grid=(DEC_BATCH,),
        in_specs=[
            per_seq(ROWS_A, KV_LORA), per_seq(ROWS_A, ROPE_DIM), per_seq(ROWS_A, HEAD_DIM_B),
            per_seq(T_PAD, S_ALL),
            per_seq(T_PAD, KV_LORA), per_seq(T_PAD, ROPE_DIM),
            per_seq(T_PAD, HEAD_DIM_B), per_seq(T_PAD, HEAD_DIM_B),
            any_spec, any_spec, any_spec, any_spec,
        ],
        out_specs=(per_seq(ROWS_A, KV_LORA), per_seq(ROWS_A, HEAD_DIM_B)),
        scratch_shapes=[
            pltpu.VMEM((2, CHUNK, KV_LORA), F32),
            pltpu.VMEM((2, ROPE_DIM, CHUNK), F32),
            pltpu.VMEM((2, CHUNK, HEAD_DIM_B), F32),
            pltpu.VMEM((2, CHUNK, HEAD_DIM_B), F32),
            pltpu.SemaphoreType.DMA((2,)),
        ],
    )
    return pl.pallas_call(
        _sample_attn_kernel,
        out_shape=(jax.ShapeDtypeStruct((DEC_BATCH, ROWS_A, KV_LORA), F32),
                   jax.ShapeDtypeStruct((DEC_BATCH, ROWS_A, HEAD_DIM_B), F32)),
        grid_spec=grid_spec,
        compiler_params=pltpu.CompilerParams(
            dimension_semantics=("arbitrary",), vmem_limit_bytes=VMEM_LIMIT),
        name="sample_attn",
    )(page_table, layer, ql, qr, qb, bias, ckvn, krn, kbn, vbn, c_ckv, c_kr, c_k, c_v)


TM_OUT = 256
N_PT_OUT = N_PROMPT // TM_OUT
N_ST_OUT = N_SAMPLE // TM_OUT


def _silu(x):
    return x * (1.0 / (1.0 + jnp.exp(-x)))


def _combine_kernel(olp_ref, ols_ref, obp_ref, obs_ref, ga_ref, gb_ref, x_ref, wuv_ref, wout_ref,
                    gfin_ref, y_ref, mix_sc, *, final):
    is_prompt = pl.program_id(0) < N_PT_OUT

    def fill(ol_ref, ob_ref):
        for h in range(H_A):
            oa = jnp.dot(ol_ref[:, h * KV_LORA:(h + 1) * KV_LORA], wuv_ref[h],
                         preferred_element_type=F32)
            gate = _silu(ga_ref[:, h * V_DIM:(h + 1) * V_DIM])
            mix_sc[:, h * V_DIM:(h + 1) * V_DIM] = (oa * gate).astype(BF16)
        mix_sc[:, H_A * V_DIM:] = (ob_ref[...] * _silu(gb_ref[...])).astype(BF16)

    @pl.when(is_prompt)
    def _():
        fill(olp_ref, obp_ref)

    @pl.when(jnp.logical_not(is_prompt))
    def _():
        fill(ols_ref, obs_ref)

    y = x_ref[...] + jnp.dot(mix_sc[...], wout_ref[...], preferred_element_type=F32)
    if final:
        y = _rms(y, gfin_ref[...])
    y_ref[...] = y


def _combine(olp, ols, obp, obs, z, x, wuv, wout, gfin, final):
    p_idx = lambda i: (jnp.minimum(i, N_PT_OUT - 1), 0)
    s_idx = lambda i: (jnp.clip(i - N_PT_OUT, 0, N_ST_OUT - 1), 0)
    return pl.pallas_call(
        functools.partial(_combine_kernel, final=final),
        out_shape=jax.ShapeDtypeStruct((N_ROWS, D_MODEL), F32),
        grid=(N_ROWS // TM_OUT,),
        in_specs=[
            pl.BlockSpec((TM_OUT, H_A * KV_LORA), p_idx),
            pl.BlockSpec((TM_OUT, H_A * KV_LORA), s_idx),
            pl.BlockSpec((TM_OUT, H_B * HEAD_DIM_B), p_idx),
            pl.BlockSpec((TM_OUT, H_B * HEAD_DIM_B), s_idx),
            pl.BlockSpec((TM_OUT, 1024), lambda i: (i, OFF_GA // 1024)),
            pl.BlockSpec((TM_OUT, 1024), lambda i: (i, OFF_GB // 1024)),
            pl.BlockSpec((TM_OUT, D_MODEL), lambda i: (i, 0)),
            pl.BlockSpec((H_A, KV_LORA, V_DIM), lambda i: (0, 0, 0)),
            pl.BlockSpec((H_A * V_DIM + H_B * HEAD_DIM_B, D_MODEL), lambda i: (0, 0)),
            pl.BlockSpec((1, D_MODEL), lambda i: (0, 0)),
        ],
        out_specs=pl.BlockSpec((TM_OUT, D_MODEL), lambda i: (i, 0)),
        scratch_shapes=[pltpu.VMEM((TM_OUT, H_A * V_DIM + H_B * HEAD_DIM_B), BF16)],
        compiler_params=pltpu.CompilerParams(
            dimension_semantics=("parallel",), vmem_limit_bytes=VMEM_LIMIT),
        name="combine_final" if final else "combine",
    )(olp, ols, obp, obs, z, z, x, wuv, wout, gfin)


def _rope_tables(pos):
    posf = pos.astype(F32)

    def cs(half, theta):
        freqs = jnp.power(jnp.float32(theta), -jnp.arange(half, dtype=F32) / half)
        ang = posf[:, None] * freqs[None, :]
        return jnp.cos(ang), jnp.sin(ang)

    n = pos.shape[0]
    zeros = lambda w: jnp.zeros((n, w), F32)
    ones = lambda w: jnp.ones((n, w), F32)
    cat = lambda xs: jnp.concatenate(xs, axis=-1)

    cm, sm = cs(ROPE_DIM // 2, THETA_MLA)
    c_m64, sa_m64, sb_m64 = cat([cm, cm]), cat([-sm, zeros(32)]), cat([zeros(32), sm])
    cb, sb = cs(ROT_B // 2, THETA_PARTIAL)
    c_b = cat([cb, cb, ones(96)])
    sa_b = cat([-sb, zeros(112)])
    sb_b = cat([zeros(16), sb, zeros(96)])
    cx, sx = cs(ROT_IDX // 2, THETA_PARTIAL)
    c_x64 = cat([cx, cx, ones(48)])
    sa_x64 = cat([-sx, zeros(56)])
    sb_x64 = cat([zeros(8), sx, zeros(48)])
    twice = lambda t: cat([t, t])
    tabs = [
        twice(c_m64), twice(sa_m64), twice(sb_m64),
        c_b, sa_b, sb_b,
        twice(c_x64), twice(sa_x64), twice(sb_x64),
        cat([c_m64, c_x64]),
        cat([sa_m64, zeros(64)]), cat([sb_m64, zeros(64)]),
        cat([zeros(64), sa_x64]), cat([zeros(64), sb_x64]),
    ]
    return jnp.stack(tabs)


def _pack_w_in(w):
    cq, ckv, kr, ga, qb, kb, vb, iq, ik, iw, gb = jnp.split(
        w, [512, 768, 832, 1856, 2880, 3008, 3136, 4160, 4224, 4240], axis=-1)
    pad = jnp.zeros((w.shape[0], 112), w.dtype)
    return jnp.concatenate([cq, ckv, kr, ik, iw, pad, qb, iq, ga, gb, kb, vb], axis=-1).astype(BF16)


def _pack_w_uq(w):
    w = w.reshape(Q_LORA, H_A, NOPE_DIM + ROPE_DIM)
    nope = w[:, :, :NOPE_DIM].reshape(Q_LORA, H_A * NOPE_DIM)
    rope = w[:, :, NOPE_DIM:].reshape(Q_LORA, H_A * ROPE_DIM)
    return jnp.concatenate([nope, rope], axis=-1).astype(BF16)


def _sample_rows(a, nh, lo, hi):
    a = a[N_PROMPT // QB:, :, :, lo:hi]
    a = a.transpose(1, 0, 2, 3).reshape(nh, DEC_BATCH, DEC_SEQ, hi - lo)
    a = jnp.pad(a.transpose(1, 0, 2, 3), ((0, 0), (0, 0), (0, T_PAD - DEC_SEQ), (0, 0)))
    return a.reshape(DEC_BATCH, nh * T_PAD, hi - lo)


def _new_rows(a, lo, hi):
    a = a[N_PROMPT:, lo:hi].reshape(DEC_BATCH, DEC_SEQ, hi - lo)
    return jnp.pad(a, ((0, 0), (0, T_PAD - DEC_SEQ), (0, 0)))


def _from_seq_rows(a, nh):
    w = a.shape[-1]
    a = a.reshape(DEC_BATCH, nh, T_PAD, w)[:, :, :DEC_SEQ]
    return a.transpose(0, 2, 1, 3).reshape(N_SAMPLE, nh * w)


def kernel(x_prompt, x_sample, cache_mla_ckv, cache_mla_krope, cache_dsa_k, cache_dsa_v, cache_dsa_idxk,
           page_table, g_attn, w_in, g_q, w_uq, g_kv, w_uk, w_uv, w_out, g_final):
    x = jnp.concatenate([x_prompt.reshape(N_PROMPT, D_MODEL), x_sample.reshape(N_SAMPLE, D_MODEL)])
    pos = jnp.concatenate([jnp.tile(jnp.arange(SEQ), BATCH),
                           jnp.tile(PAST_LEN + jnp.arange(DEC_SEQ), DEC_BATCH)])
    tabs = _rope_tables(pos)
    gfin = g_final.reshape(1, D_MODEL)
    cache_kr_t = jnp.swapaxes(cache_mla_krope, 2, 3)
    cache_ik_t = jnp.swapaxes(cache_dsa_idxk, 2, 3)

    rows = []
    for l in range(DEPTH):
        final = l == DEPTH - 1
        layer = jnp.full((1,), l, I32)
        z = _in_proj(x, g_attn[l].reshape(1, D_MODEL), _pack_w_in(w_in[l]))
        (ckv, krik, kb, vb, ckvb, krikb, kbb, vbb, ql, qr, qb, iq, iw) = _post_proj(
            z, tabs, g_q[l].reshape(1, Q_LORA), g_kv[l].reshape(1, KV_LORA),
            _pack_w_uq(w_uq[l]), w_uk[l].astype(BF16))
        rows.append((ckv, krik[:, :ROPE_DIM], kb, vb, krik[:, ROPE_DIM:]))

        ol_p = _mla_prompt(ql, qr, ckvb, krikb)
        ob_p = _dsa_prompt(iq, iw, qb, krikb, kbb, vbb)

        iw_s = iw[N_PROMPT:, :N_IDX].reshape(DEC_BATCH, DEC_SEQ, N_IDX).transpose(0, 2, 1)
        iw_s = jnp.pad(iw_s, ((0, 0), (0, 0), (0, T_PAD - DEC_SEQ))).reshape(DEC_BATCH, ROWS_I, 1)
        scores = _sample_index(page_table, layer, _sample_rows(iq, N_IDX, D_IDX, LANES), iw_s,
                               _new_rows(krik, ROPE_DIM, LANES), cache_ik_t)
        ol_s, ob_s = _sample_attn(
            page_table, layer,
            _sample_rows(ql, H_A, 0, KV_LORA), _sample_rows(qr, H_A, 0, ROPE_DIM),
            _sample_rows(qb, H_B, 0, HEAD_DIM_B), _sample_topk(scores),
            _new_rows(ckv, 0, KV_LORA), _new_rows(krik, 0, ROPE_DIM),
            _new_rows(kb, 0, HEAD_DIM_B), _new_rows(vb, 0, HEAD_DIM_B),
            cache_mla_ckv, cache_kr_t, cache_dsa_k, cache_dsa_v)
        x = _combine(ol_p, _from_seq_rows(ol_s, H_A).astype(BF16), ob_p, _from_seq_rows(ob_s, H_B),
                     z, x, w_uv[l].astype(BF16), w_out[l].astype(BF16), gfin, final)

    y_prompt = x[:N_PROMPT].reshape(BATCH, SEQ, D_MODEL)
    y_sample = x[N_PROMPT:].reshape(DEC_BATCH, DEC_SEQ, D_MODEL)

    def stack(k, lo, hi, shape):
        return jnp.stack([r[k][lo:hi].reshape(shape + (r[k].shape[-1],)) for r in rows])

    p_out = [stack(k, 0, N_PROMPT, (BATCH, SEQ)) for k in range(5)]
    s_out = [stack(k, N_PROMPT, N_ROWS, (DEC_BATCH, DEC_SEQ)) for k in range(5)]
    return (y_prompt, y_sample, *p_out, *s_out)
```
